```python
import functools
import jax
import jax.numpy as jnp
from jax import lax
import numpy as np

D_MODEL = 1024
BATCH = 16
SEQ = 256
DEPTH = 2
DEC_BATCH = 8
DEC_SEQ = 2048
PAST_LEN = 256

GRID_W = 64
N_DIR = 2
A_HEADS = 16
A_HEAD_DIM = 64
A_WIDTH = A_HEADS * A_HEAD_DIM
DECAY_LORA = 64
ICLR_LORA = 64
GATE_LORA = 128
A_SHIFTED = 3 * A_WIDTH + DECAY_LORA + ICLR_LORA
A_COLS = A_SHIFTED + GATE_LORA
LNX_EPS = 64e-5
B_WIDTH = D_MODEL
B_BLOCKS = 8
B_BLOCK = B_WIDTH // B_BLOCKS
CONV_W = 4
LRU_C = 8.0
N_BRANCH = 2
IN_COLS = A_COLS + 2 * B_WIDTH + N_BRANCH * D_MODEL
DENSE_FF = 4 * D_MODEL
N_EXPERTS = 8
TOP_K = 2
EXPERT_FF = 7 * D_MODEL // 2
N_DENSE = (DEPTH + 1) // 2
N_MOE = DEPTH // 2
N_MOD = 6
EPS = 1e-6

kernel_name = "bidir_rwkv7_rglru_diffusion_step"


def rmsnorm(x, g):
    xf = x.astype(jnp.float32)
    y = xf * lax.rsqrt(jnp.mean(xf * xf, axis=-1, keepdims=True) + EPS)
    return (y * g.astype(jnp.float32)).astype(x.dtype)


def grid_rows(x, rows):
    b, l, ch = x.shape
    return x.reshape(b, rows, l // rows, ch)


def shift_prev(x, rows):
    xr = grid_rows(x, rows)
    return jnp.pad(xr[:, :, :-1], ((0, 0), (0, 0), (1, 0), (0, 0))).reshape(x.shape)


def shift_next(x, rows):
    xr = grid_rows(x, rows)
    return jnp.pad(xr[:, :, 1:], ((0, 0), (0, 0), (0, 1), (0, 0))).reshape(x.shape)


def dwconv_centred(x, w, b, rows):
    xr = grid_rows(x, rows)
    n = xr.shape[2]
    left = CONV_W // 2
    xp = jnp.pad(xr, ((0, 0), (0, 0), (left, CONV_W - 1 - left), (0, 0)))
    y = xp[:, :, 0:n] * w[0]
    for j in range(1, CONV_W):
        y = y + xp[:, :, j:j + n] * w[j]
    return (y + b).reshape(x.shape)


def to_scan(z):
    z = jnp.stack([z[0], jnp.flip(z[1], axis=1)])
    return jnp.moveaxis(z, 2, 0)


def from_scan(z):
    z = jnp.moveaxis(z, 0, 2)
    return jnp.stack([z[0], jnp.flip(z[1], axis=1)])


def rwkv7_mix(pa, rows, s0, mu, w0, w_up, a0, a_up, k_k, k_a, r_k, g_up, lnx_w, lnx_b):
    f32 = jnp.float32
    pa = pa.astype(f32)
    b, l, _ = pa.shape
    ps, gd = pa[..., :A_SHIFTED], pa[..., A_SHIFTED:]
    shifted = jnp.stack([shift_prev(ps, rows), shift_next(ps, rows)])
    xd = ps + (shifted - ps) * mu[:, None, None, :]
    r, k, v, wd, ad = jnp.split(
        xd, [A_WIDTH, 2 * A_WIDTH, 3 * A_WIDTH, 3 * A_WIDTH + DECAY_LORA], axis=-1)
    w_log = -jax.nn.softplus(-(w0[:, None, None, :]
                               + jnp.einsum("dblr,drc->dblc", jnp.tanh(wd), w_up))) - 0.5
    decay = jnp.exp(-jnp.exp(w_log))
    a = jax.nn.sigmoid(a0[:, None, None, :] + jnp.einsum("dblr,drc->dblc", ad, a_up))

    def heads(z):
        return z.reshape(N_DIR, b, l, A_HEADS, A_HEAD_DIM)

    kk = heads(k * k_k)
    kk = kk / jnp.maximum(jnp.linalg.norm(kk, axis=-1, keepdims=True), 1e-12)
    k = heads(k * (1.0 + (a - 1.0) * k_a))
    r, v, decay, a = heads(r), heads(v), heads(decay), heads(a)

    def step(S, inp):
        r_t, k_t, v_t, w_t, kk_t, a_t = inp
        s_kk = jnp.einsum("dbhij,dbhj->dbhi", S, kk_t)
        S = (S * w_t[..., None, :]
             - s_kk[..., None] * (kk_t * a_t)[..., None, :]
             + v_t[..., None] * k_t[..., None, :])
        return S, jnp.einsum("dbhij,dbhj->dbhi", S, r_t)

    s_final, ys = lax.scan(step, jnp.moveaxis(s0.astype(f32), 1, 0),
                           tuple(to_scan(z) for z in (r, k, v, decay, kk, a)))
    y = from_scan(ys).sum(0)
    mean = jnp.mean(y, axis=-1, keepdims=True)
    var = jnp.mean(jnp.square(y - mean), axis=-1, keepdims=True)
    y = ((y - mean) * lax.rsqrt(var + LNX_EPS)).reshape(b, l, A_WIDTH) * lnx_w + lnx_b
    bonus = jnp.sum(jnp.sum(r * k * r_k, axis=-1, keepdims=True) * v, axis=0).reshape(b, l, A_WIDTH)
    g = jax.nn.sigmoid(gd) @ g_up
    return (y + bonus) * g, jnp.moveaxis(s_final, 0, 1)


def rglru_mix(pb, rows, h0, conv_w, conv_b, w_rg, b_rg, w_ig, b_ig, lam):
    f32 = jnp.float32
    pb = pb.astype(f32)
    xin, gate = pb[..., :B_WIDTH], pb[..., B_WIDTH:]
    xc = dwconv_centred(xin, conv_w, conv_b, rows)
    b, l, _ = xc.shape
    xb = xc.reshape(b, l, B_BLOCKS, B_BLOCK)

    def block_diag(w, bias):
        return (jnp.einsum("blnc,dnce->dblne", xb, w).reshape(N_DIR, b, l, B_WIDTH)
                + bias[:, None, None, :])

    rg = jax.nn.sigmoid(block_diag(w_rg, b_rg))
    ig = jax.nn.sigmoid(block_diag(w_ig, b_ig))
    log_a = -LRU_C * rg * jax.nn.softplus(-lam)[:, None, None, :]
    a = jnp.exp(log_a)
    u = jnp.sqrt(-jnp.expm1(2.0 * log_a)) * ig * xc

    def combine(e1, e2):
        return e1[0] * e2[0], e2[0] * e1[1] + e2[1]

    a_cum, u_cum = lax.associative_scan(combine, (to_scan(a), to_scan(u)), axis=0)
    h = a_cum * jnp.moveaxis(h0.astype(f32), 1, 0) + u_cum
    y = from_scan(h).sum(0) * jax.nn.gelu(gate)
    return y, jnp.moveaxis(h[-1], 0, 1)


def token_mix(h, rows, sa0, sb0, lp):
    p = h @ lp["w_in"]
    pa = p[..., :A_COLS]
    pb = p[..., A_COLS:A_COLS + 2 * B_WIDTH]
    pg = p[..., A_COLS + 2 * B_WIDTH:]
    ya, sa = rwkv7_mix(pa, rows, sa0, lp["rwkv_mu"], lp["rwkv_w0"], lp["rwkv_w_up"],
                       lp["rwkv_a0"], lp["rwkv_a_up"], lp["rwkv_k_k"], lp["rwkv_k_a"],
                       lp["rwkv_r_k"], lp["rwkv_g_up"], lp["rwkv_lnx_w"], lp["rwkv_lnx_b"])
    yb, sb = rglru_mix(pb, rows, sb0, lp["lru_conv_w"], lp["lru_conv_b"], lp["lru_w_rg"],
                       lp["lru_b_rg"], lp["lru_w_ig"], lp["lru_b_ig"], lp["lru_lam"])
    gates = jax.nn.sigmoid((pg + lp["b_merge"]).astype(jnp.float32))
    ga, gb = gates[..., :D_MODEL], gates[..., D_MODEL:]
    m = (ga * (ya.astype(h.dtype) @ lp["w_proj_a"])
         + gb * (yb.astype(h.dtype) @ lp["w_proj_b"]))
    return m.astype(h.dtype) @ lp["w_out"], sa, sb


def swiglu(h, w1, w3, w2):
    return (jax.nn.silu(h @ w1) * (h @ w3)) @ w2


def moe_swiglu(h, router, w1, w3, w2):
    f32 = jnp.float32
    logits = (h @ router).astype(f32)
    top_v, top_i = lax.top_k(logits, TOP_K)
    top_w = jax.nn.softmax(top_v, axis=-1)
    gates = jnp.sum(jax.nn.one_hot(top_i, N_EXPERTS, dtype=f32) * top_w[..., None], axis=-2)
    out = jnp.zeros(h.shape, f32)
    for e in range(N_EXPERTS):
        out = out + gates[..., e:e + 1] * swiglu(h, w1[e], w3[e], w2[e]).astype(f32)
    return out.astype(h.dtype)


def layer(x, cond, rows, sa0, sb0, lp, channel_mix):
    mod = jax.nn.silu(cond) @ lp["mod_w"] + lp["mod_b"]
    sh1, sc1, g1, sh2, sc2, g2 = jnp.split(mod[:, None, :], N_MOD, axis=-1)
    h = rmsnorm(x, lp["norm_pre_mix"]) * (1.0 + sc1) + sh1
    m, sa, sb = token_mix(h, rows, sa0, sb0, lp)
    x = x + g1 * rmsnorm(m, lp["norm_post_mix"])
    h = rmsnorm(x, lp["norm_pre_ffn"]) * (1.0 + sc2) + sh2
    x = x + g2 * rmsnorm(channel_mix(h), lp["norm_post_ffn"])
    return x, sa, sb


def setup_inputs(seed: int = 0) -> dict:
    key = jax.random.key(seed)
    keys = jax.random.split(key, 64)
    ks = (keys[i] for i in range(64))
    f32 = jnp.float32

    def nrm(shape, scale):
        return jax.random.normal(next(ks), shape, f32) * scale

    def unif(shape, lo, hi):
        return jax.random.uniform(next(ks), shape, f32, lo, hi)

    def gain(shape):
        return 1.0 + nrm(shape, 0.05)

    L, DR, E = DEPTH, N_DIR, N_EXPERTS
    a_init = unif((L, DR, B_WIDTH), 0.9, 0.999)
    a_root = a_init ** (1.0 / LRU_C)
    return {
        "x_prompt": nrm((BATCH, SEQ, D_MODEL), 1.0),
        "x_sample": nrm((DEC_BATCH, DEC_SEQ, D_MODEL), 1.0),
        "c": nrm((DEC_BATCH, D_MODEL), 1.0),
        "c_ctx": nrm((D_MODEL,), 1.0),
        "state_rwkv": nrm((DEC_BATCH, DEPTH, N_DIR, A_HEADS, A_HEAD_DIM, A_HEAD_DIM), 0.5),
        "state_lru": nrm((DEC_BATCH, DEPTH, N_DIR, B_WIDTH), 0.5),
        "mod_w": nrm((L, D_MODEL, N_MOD * D_MODEL), 0.5 * D_MODEL ** -0.5),
        "mod_b": nrm((L, N_MOD * D_MODEL), 0.02),
        "norm_pre_mix": gain((L, D_MODEL)),
        "norm_post_mix": gain((L, D_MODEL)),
        "norm_pre_ffn": gain((L, D_MODEL)),
        "norm_post_ffn": gain((L, D_MODEL)),
        "w_in": nrm((L, D_MODEL, IN_COLS), D_MODEL ** -0.5),
        "b_merge": nrm((L, N_BRANCH * D_MODEL), 0.02),
        "rwkv_mu": unif((L, DR, A_SHIFTED), 0.0, 1.0),
        "rwkv_w0": unif((L, DR, A_WIDTH), -6.0, -1.0),
        "rwkv_w_up": nrm((L, DR, DECAY_LORA, A_WIDTH), 0.5 * DECAY_LORA ** -0.5),
        "rwkv_a0": nrm((L, DR, A_WIDTH), 0.5),
        "rwkv_a_up": nrm((L, DR, ICLR_LORA, A_WIDTH), 0.5 * ICLR_LORA ** -0.5),
        "rwkv_k_k": 0.85 + nrm((L, A_WIDTH), 0.05),
        "rwkv_k_a": gain((L, A_WIDTH)),
        "rwkv_r_k": nrm((L, A_HEADS, A_HEAD_DIM), 0.1),
        "rwkv_g_up": nrm((L, GATE_LORA, A_WIDTH), GATE_LORA ** -0.5),
        "rwkv_lnx_w": gain((L, A_WIDTH)),
        "rwkv_lnx_b": nrm((L, A_WIDTH), 0.02),
        "lru_conv_w": nrm((L, CONV_W, B_WIDTH), CONV_W ** -0.5),
        "lru_conv_b": nrm((L, B_WIDTH), 0.02),
        "lru_w_rg": nrm((L, DR, B_BLOCKS, B_BLOCK, B_BLOCK), B_BLOCK ** -0.5),
        "lru_b_rg": nrm((L, DR, B_WIDTH), 0.02),
        "lru_w_ig": nrm((L, DR, B_BLOCKS, B_BLOCK, B_BLOCK), B_BLOCK ** -0.5),
        "lru_b_ig": nrm((L, DR, B_WIDTH), 0.02),
        "lru_lam": jnp.log(a_root) - jnp.log1p(-a_root),
        "w_proj_a": nrm((L, A_WIDTH, D_MODEL), A_WIDTH ** -0.5),
        "w_proj_b": nrm((L, B_WIDTH, D_MODEL), B_WIDTH ** -0.5),
        "w_out": nrm((L, D_MODEL, D_MODEL), D_MODEL ** -0.5),
        "ffn_w1": nrm((N_DENSE, D_MODEL, DENSE_FF), D_MODEL ** -0.5),
        "ffn_w3": nrm((N_DENSE, D_MODEL, DENSE_FF), D_MODEL ** -0.5),
        "ffn_w2": nrm((N_DENSE, DENSE_FF, D_MODEL), DENSE_FF ** -0.5),
        "moe_router": nrm((N_MOE, D_MODEL, N_EXPERTS), D_MODEL ** -0.5),
        "moe_w1": nrm((N_MOE, E, D_MODEL, EXPERT_FF), D_MODEL ** -0.5),
        "moe_w3": nrm((N_MOE, E, D_MODEL, EXPERT_FF), D_MODEL ** -0.5),
        "moe_w2": nrm((N_MOE, E, EXPERT_FF, D_MODEL), EXPERT_FF ** -0.5),
    }


def reference(x_prompt, x_sample, c, c_ctx, state_rwkv, state_lru,
              mod_w, mod_b, norm_pre_mix, norm_post_mix, norm_pre_ffn, norm_post_ffn,
              w_in, b_merge,
              rwkv_mu, rwkv_w0, rwkv_w_up, rwkv_a0, rwkv_a_up, rwkv_k_k, rwkv_k_a,
              rwkv_r_k, rwkv_g_up, rwkv_lnx_w, rwkv_lnx_b,
              lru_conv_w, lru_conv_b, lru_w_rg, lru_b_rg, lru_w_ig, lru_b_ig, lru_lam,
              w_proj_a, w_proj_b, w_out,
              ffn_w1, ffn_w3, ffn_w2,
              moe_router, moe_w1, moe_w3, moe_w2):
    rows_sample = x_sample.shape[1] // GRID_W
    rows_ctx = 1
    b_p = x_prompt.shape[0]
    zero_rwkv = jnp.zeros((b_p, N_DIR, A_HEADS, A_HEAD_DIM, A_HEAD_DIM), jnp.float32)
    zero_lru = jnp.zeros((b_p, N_DIR, B_WIDTH), jnp.float32)
    cond_ctx = jnp.broadcast_to(c_ctx, (b_p, D_MODEL))

    y_prompt, y_sample = x_prompt, x_sample
    ctx_rwkv, ctx_lru = [], []
    for l in range(DEPTH):
        lp = dict(
            mod_w=mod_w[l], mod_b=mod_b[l],
            norm_pre_mix=norm_pre_mix[l], norm_post_mix=norm_post_mix[l],
            norm_pre_ffn=norm_pre_ffn[l], norm_post_ffn=norm_post_ffn[l],
            w_in=w_in[l], b_merge=b_merge[l],
            rwkv_mu=rwkv_mu[l], rwkv_w0=rwkv_w0[l], rwkv_w_up=rwkv_w_up[l],
            rwkv_a0=rwkv_a0[l], rwkv_a_up=rwkv_a_up[l], rwkv_k_k=rwkv_k_k[l],
            rwkv_k_a=rwkv_k_a[l], rwkv_r_k=rwkv_r_k[l], rwkv_g_up=rwkv_g_up[l],
            rwkv_lnx_w=rwkv_lnx_w[l], rwkv_lnx_b=rwkv_lnx_b[l],
            lru_conv_w=lru_conv_w[l], lru_conv_b=lru_conv_b[l],
            lru_w_rg=lru_w_rg[l], lru_b_rg=lru_b_rg[l],
            lru_w_ig=lru_w_ig[l], lru_b_ig=lru_b_ig[l], lru_lam=lru_lam[l],
            w_proj_a=w_proj_a[l], w_proj_b=w_proj_b[l], w_out=w_out[l],
        )
        i = l // 2
        if l % 2 == 0:
            channel_mix = functools.partial(swiglu, w1=ffn_w1[i], w3=ffn_w3[i], w2=ffn_w2[i])
        else:
            channel_mix = functools.partial(moe_swiglu, router=moe_router[i],
                                            w1=moe_w1[i], w3=moe_w3[i], w2=moe_w2[i])
        y_prompt, s_rwkv, s_lru = layer(y_prompt, cond_ctx, rows_ctx, zero_rwkv, zero_lru,
                                        lp, channel_mix)
        ctx_rwkv.append(s_rwkv)
        ctx_lru.append(s_lru)
        y_sample, _, _ = layer(y_sample, c, rows_sample, state_rwkv[:, l], state_lru[:, l],
                               lp, channel_mix)

    new_state_rwkv = jnp.stack(ctx_rwkv, axis=1).astype(x_prompt.dtype)
    new_state_lru = jnp.stack(ctx_lru, axis=1).astype(x_prompt.dtype)
    return (y_prompt, y_sample, new_state_rwkv, new_state_lru)
```

```python
import functools

import numpy as np
import jax
import jax.numpy as jnp
from jax import lax
from jax.experimental import pallas as pl
from jax.experimental.pallas import tpu as pltpu

F32 = jnp.float32
BF16 = jnp.bfloat16

GRID_W = 64
CHUNK = 64
LANES = 128
SUBLANES = 8
HEAD_DIM = 64
LRU_C = 8.0
EPS = 1e-6
LNX_EPS = 64e-5
N_MOD = 6
VMEM_LIMIT = 56 * 1024 * 1024


def _dot(a, b):
    return jnp.dot(a, b, preferred_element_type=F32)


def _dot_nt(a, b):
    return lax.dot_general(a, b, (((1,), (1,)), ((), ())), preferred_element_type=F32)


def _dot_tn(a, b):
    return lax.dot_general(a, b, (((0,), (0,)), ((), ())), preferred_element_type=F32)


def _split2(x):
    hi = x.astype(BF16)
    lo = (x - hi.astype(F32)).astype(BF16)
    return hi, lo


def _dot3(a, b):
    ah, al = _split2(a)
    bh, bl = _split2(b)
    return _dot(ah, bh) + _dot(ah, bl) + _dot(al, bh)


def _sigmoid(x):
    return 1.0 / (1.0 + jnp.exp(-x))


def _softplus(x):
    return jnp.maximum(x, 0.0) + jnp.log(1.0 + jnp.exp(-jnp.abs(x)))


def _silu(x):
    return x * _sigmoid(x)


def _gelu_tanh(x):
    return 0.5 * x * (1.0 + jnp.tanh(0.7978845608028654 * (x + 0.044715 * (x * x * x))))


def _rmsnorm(x, g):
    ms = jnp.mean(x * x, axis=-1, keepdims=True)
    return x * lax.rsqrt(ms + EPS) * g


def _seg_sum(x, e_ref, et_ref):
    hi, lo = _split2(x)
    hs = _dot(hi, e_ref[...]) + _dot(lo, e_ref[...])
    hh, hl = _split2(hs)
    return _dot(hh, et_ref[...]) + _dot(hl, et_ref[...])


def _params(sem):
    return pltpu.CompilerParams(dimension_semantics=sem, vmem_limit_bytes=VMEM_LIMIT)


def _mod_kernel(c_ref, w_ref, b_ref, o_ref):
    s = _silu(c_ref[...])
    o_ref[...] = _dot3(s, w_ref[...]) + b_ref[...]


def _modulation(cond, mod_w, mod_b):
    nl, d, n = mod_w.shape
    rows = cond.shape[0]
    tn = d
    return pl.pallas_call(
        _mod_kernel,
        grid=(nl, n // tn),
        in_specs=[
            pl.BlockSpec((rows, d), lambda l, j: (0, 0)),
            pl.BlockSpec((None, d, tn), lambda l, j: (l, 0, j)),
            pl.BlockSpec((None, 1, tn), lambda l, j: (l, 0, j)),
        ],
        out_specs=pl.BlockSpec((None, rows, tn), lambda l, j: (l, 0, j)),
        out_shape=jax.ShapeDtypeStruct((nl, rows, n), F32),
        compiler_params=_params(("arbitrary", "arbitrary")),
        name="modulation",
    )(cond, mod_w, mod_b.reshape(nl, 1, n))


def _inproj_kernel(x_ref, mod_ref, g_ref, w_ref, pa_ref, pb_ref, pg_ref, *, na, nb):
    h = _rmsnorm(x_ref[...], g_ref[...])
    h = h * (1.0 + mod_ref[0, 1:2, :]) + mod_ref[0, 0:1, :]
    hb = h.astype(BF16)
    pa_ref[...] = _dot(hb, w_ref[:, :na])
    pb_ref[...] = _dot(hb, w_ref[:, na:na + nb])
    pg_ref[...] = _dot(hb, w_ref[:, na + nb:])


def _inproj(x, mod_t, g, w, na, nb, tm):
    t, d = x.shape
    n = w.shape[1]
    ng = n - na - nb
    return pl.pallas_call(
        functools.partial(_inproj_kernel, na=na, nb=nb),
        grid=(t // tm,),
        in_specs=[
            pl.BlockSpec((tm, d), lambda i: (i, 0)),
            pl.BlockSpec((1, N_MOD, d), lambda i: (i, 0, 0)),
            pl.BlockSpec((1, d), lambda i: (0, 0)),
            pl.BlockSpec((d, n), lambda i: (0, 0)),
        ],
        out_specs=[
            pl.BlockSpec((tm, na), lambda i: (i, 0)),
            pl.BlockSpec((tm, nb), lambda i: (i, 0)),
            pl.BlockSpec((tm, ng), lambda i: (i, 0)),
        ],
        out_shape=[
            jax.ShapeDtypeStruct((t, na), F32),
            jax.ShapeDtypeStruct((t, nb), F32),
            jax.ShapeDtypeStruct((t, ng), F32),
        ],
        compiler_params=_params(("arbitrary",)),
        name="inproj",
    )(x, mod_t, g, w)


def _scan_table(seq_lens, row_lens, reverse):
    rows = [[] for _ in range(6)]
    base = 0
    for s, (ls, rl) in enumerate(zip(seq_lens, row_lens)):
        assert ls % CHUNK == 0 and rl % CHUNK == 0 and ls % rl == 0
        nc = ls // CHUNK
        order = range(nc - 1, -1, -1) if reverse else range(nc)
        for n, ci in enumerate(order):
            t0 = ci * CHUNK
            rows[0].append(base + ci)
            rows[1].append(int(n == 0))
            rows[2].append(int(n == nc - 1))
            rows[3].append(s)
            rows[4].append(int(t0 % rl != 0))
            rows[5].append(int((t0 + CHUNK) % rl != 0))
        base += nc
    return jnp.asarray(np.array(rows, dtype=np.int32))


def _halo_prev(g, tab):
    return (jnp.maximum(tab[0, g] * (CHUNK // SUBLANES) - 1, 0), 0)


def _halo_next(n_blocks8):
    def index_map(g, tab):
        return (jnp.minimum((tab[0, g] + 1) * (CHUNK // SUBLANES), n_blocks8 - 1), 0)
    return index_map


def _rwkv_kernel(tab_ref, pa_ref, halo_ref, s0_ref, mu_ref, w0_ref, wup_ref, a0_ref, aup_ref,
                 kk_ref, ka_ref, rk_ref, e_ref, et_ref,
                 y_ref, bonus_ref, sfin_ref, s_scr, *, reverse, width, n_shift):
    g = pl.program_id(0)
    n_pairs = width // LANES

    @pl.when(tab_ref[1, g] == 1)
    def _():
        s_scr[...] = s0_ref[0]

    row = lax.broadcasted_iota(jnp.int32, (CHUNK, 1), 0)
    if reverse:
        edge = row == CHUNK - 1
        halo_ok = tab_ref[5, g].astype(F32)
        halo_row, shift = 0, CHUNK - 1
    else:
        edge = row == 0
        halo_ok = tab_ref[4, g].astype(F32)
        halo_row, shift = SUBLANES - 1, 1

    def mix(lo, hi):
        ps = pa_ref[:, lo:hi]
        nb = halo_ref[halo_row:halo_row + 1, lo:hi] * halo_ok
        shifted = jnp.where(edge, nb, pltpu.roll(ps, shift, 0))
        return ps + (shifted - ps) * mu_ref[:, lo:hi]

    r = mix(0, width)
    k = mix(width, 2 * width)
    v = mix(2 * width, 3 * width)
    wa = mix(3 * width, n_shift)

    lw = w0_ref[...] + _dot(jnp.tanh(wa).astype(BF16), wup_ref[...])
    logw = -jnp.exp(-_softplus(-lw) - 0.5)
    a = _sigmoid(a0_ref[...] + _dot(wa.astype(BF16), aup_ref[...]))

    kk = k * kk_ref[...]
    nrm = jnp.sqrt(_seg_sum(kk * kk, e_ref, et_ref))
    kk = kk / jnp.maximum(nrm, 1e-12)
    k = k * (1.0 + (a - 1.0) * ka_ref[...])
    bonus_ref[...] = _seg_sum(r * k * rk_ref[...], e_ref, et_ref) * v

    ti = lax.broadcasted_iota(jnp.int32, (CHUNK, CHUNK), 0)
    si = lax.broadcasted_iota(jnp.int32, (CHUNK, CHUNK), 1)
    tri = jnp.where((si >= ti) if reverse else (si <= ti), 1.0, 0.0).astype(BF16)
    l1 = logw.astype(BF16)
    rem = logw - l1.astype(F32)
    l2 = rem.astype(BF16)
    l3 = (rem - l2.astype(F32)).astype(BF16)
    gc = _dot(tri, l1) + _dot(tri, l2) + _dot(tri, l3)
    g_end = gc[0:1, :] if reverse else gc[CHUNK - 1:CHUNK, :]

    beta = kk * a
    inv = jnp.exp(-gc)
    to_end = jnp.exp(g_end - gc)
    r_t = r * jnp.exp(gc)
    a_t = -kk * jnp.exp(gc - logw)
    b_t = beta * inv
    k_t = k * inv
    b_h = beta * to_end
    k_h = k * to_end
    dec_end = jnp.exp(g_end)

    head0 = lax.broadcasted_iota(jnp.int32, (1, LANES), 1) < HEAD_DIM

    def stack(x, p):
        xp = x[:, p * LANES:(p + 1) * LANES]
        return jnp.concatenate([jnp.where(head0, xp, 0.0), jnp.where(head0, 0.0, xp)],
                               axis=0).astype(BF16)

    n2 = 2 * CHUNK
    tt = lax.broadcasted_iota(jnp.int32, (n2, n2), 0) % CHUNK
    ss = lax.broadcasted_iota(jnp.int32, (n2, n2), 1) % CHUNK
    strict = (ss > tt) if reverse else (ss < tt)
    incl = (ss >= tt) if reverse else (ss <= tt)
    eye = jnp.where(lax.broadcasted_iota(jnp.int32, (n2, n2), 0)
                    == lax.broadcasted_iota(jnp.int32, (n2, n2), 1), 1.0, 0.0)

    for p in range(n_pairs):
        sl = slice(p * LANES, (p + 1) * LANES)
        la, lr = stack(a_t, p), stack(r_t, p)
        rb, rkk = stack(b_t, p), stack(k_t, p)
        vs = stack(v, p)
        prod = _dot_nt(jnp.concatenate([la, lr], axis=0), jnp.concatenate([rb, rkk], axis=0))
        a_ab = jnp.where(strict, prod[:n2, :n2], 0.0)
        a_ak = jnp.where(strict, prod[:n2, n2:], 0.0)
        a_rb = jnp.where(incl, prod[n2:, :n2], 0.0)
        a_rk = jnp.where(incl, prod[n2:, n2:], 0.0)

        inv_m = eye + a_ab
        pw = a_ab
        for _ in range(CHUNK.bit_length() - 2):
            pb = pw.astype(BF16)
            pw = _dot(pb, pb)
            inv_m = inv_m + _dot(inv_m.astype(BF16), pw.astype(BF16))

        s_old = s_scr[p]
        sb = s_old.astype(BF16)
        rhs = _dot_nt(la, sb) + _dot(a_ak.astype(BF16), vs)
        u = _dot(inv_m.astype(BF16), rhs.astype(BF16))
        ub = u.astype(BF16)
        yy = _dot_nt(lr, sb) + _dot(a_rb.astype(BF16), ub) + _dot(a_rk.astype(BF16), vs)
        y_ref[:, sl] = yy[:CHUNK] + yy[CHUNK:]
        s_scr[p] = (s_old * dec_end[:, sl] + _dot_tn(ub, stack(b_h, p))
                    + _dot_tn(vs, stack(k_h, p)))

    @pl.when(tab_ref[2, g] == 1)
    def _():
        sfin_ref[0] = s_scr[...]


def _rwkv_scan(tab, pa, s0, mu, w0, wup, a0, aup, k_k, k_a, r_k, e, et, *, reverse, width,
               n_shift):
    t, na = pa.shape
    n_steps = tab.shape[1]
    n_seq, n_pairs = s0.shape[0], s0.shape[1]
    const = lambda g, tab: (0, 0)
    vec = pl.BlockSpec((1, width), const)
    halo_map = _halo_next(t // SUBLANES) if reverse else _halo_prev
    grid_spec = pltpu.PrefetchScalarGridSpec(
        num_scalar_prefetch=1,
        grid=(n_steps,),
        in_specs=[
            pl.BlockSpec((CHUNK, na), lambda g, tab: (tab[0, g], 0)),
            pl.BlockSpec((SUBLANES, na), halo_map),
            pl.BlockSpec((1, n_pairs, LANES, LANES), lambda g, tab: (tab[3, g], 0, 0, 0)),
            pl.BlockSpec((1, n_shift), const),
            vec,
            pl.BlockSpec((LANES, width), const),
            vec,
            pl.BlockSpec((LANES, width), const),
            vec, vec, vec,
            pl.BlockSpec((width, LANES), const),
            pl.BlockSpec((LANES, width), const),
        ],
        out_specs=[
            pl.BlockSpec((CHUNK, width), lambda g, tab: (tab[0, g], 0)),
            pl.BlockSpec((CHUNK, width), lambda g, tab: (tab[0, g], 0)),
            pl.BlockSpec((1, n_pairs, LANES, LANES), lambda g, tab: (tab[3, g], 0, 0, 0)),
        ],
        scratch_shapes=[pltpu.VMEM((n_pairs, LANES, LANES), F32)],
    )
    return pl.pallas_call(
        functools.partial(_rwkv_kernel, reverse=reverse, width=width, n_shift=n_shift),
        grid_spec=grid_spec,
        out_shape=[
            jax.ShapeDtypeStruct((t, width), F32),
            jax.ShapeDtypeStruct((t, width), F32),
            jax.ShapeDtypeStruct((n_seq, n_pairs, LANES, LANES), F32),
        ],
        compiler_params=_params(("arbitrary",)),
        name="rwkv_bwd" if reverse else "rwkv_fwd",
    )(tab, pa, pa, s0, mu, w0, wup, a0, aup, k_k, k_a, r_k, e, et)


def _lru_kernel(tab_ref, pb_ref, hp_ref, hn_ref, h0_ref, cw_ref, cb_ref, wg_ref, brg_ref,
                big_ref, lam_ref, *rest, reverse, width):
    if reverse:
        hf_ref, out_ref, hfin_ref, a_scr, u_scr, h_scr, hb_scr = rest
        dst = hb_scr
    else:
        out_ref, hfin_ref, a_scr, u_scr, h_scr = rest
        dst = out_ref
    g = pl.program_id(0)

    @pl.when(tab_ref[1, g] == 1)
    def _():
        h_scr[...] = h0_ref[0]

    prev_ok = tab_ref[4, g].astype(F32)
    next_ok = tab_ref[5, g].astype(F32)
    row = lax.broadcasted_iota(jnp.int32, (CHUNK, 1), 0)
    sp = _softplus(-lam_ref[...])

    for n in range(width // LANES):
        sl = slice(n * LANES, (n + 1) * LANES)
        x = pb_ref[:, sl]
        p7 = hp_ref[SUBLANES - 1:SUBLANES, sl] * prev_ok
        p6 = hp_ref[SUBLANES - 2:SUBLANES - 1, sl] * prev_ok
        n0 = hn_ref[0:1, sl] * next_ok
        xm1 = jnp.where(row == 0, p7, pltpu.roll(x, 1, 0))
        xm2 = jnp.where(row == 0, p6, jnp.where(row == 1, p7, pltpu.roll(x, 2, 0)))
        xp1 = jnp.where(row == CHUNK - 1, n0, pltpu.roll(x, CHUNK - 1, 0))
        xc = (xm2 * cw_ref[0:1, sl] + xm1 * cw_ref[1:2, sl] + x * cw_ref[2:3, sl]
              + xp1 * cw_ref[3:4, sl] + cb_ref[:, sl])
        gz = _dot(xc.astype(BF16), wg_ref[n])
        rg = _sigmoid(gz[:, :LANES] + brg_ref[:, sl])
        ig = _sigmoid(gz[:, LANES:] + big_ref[:, sl])
        log_a = (-LRU_C) * rg * sp[:, sl]
        a_scr[:, sl] = jnp.exp(log_a)
        u_scr[:, sl] = jnp.sqrt(1.0 - jnp.exp(2.0 * log_a)) * ig * xc

    h = h_scr[...]
    for t in (range(CHUNK - 1, -1, -1) if reverse else range(CHUNK)):
        h = a_scr[t:t + 1, :] * h + u_scr[t:t + 1, :]
        dst[t:t + 1, :] = h
    h_scr[...] = h

    if reverse:
        out_ref[...] = (hf_ref[...] + hb_scr[...]) * _gelu_tanh(pb_ref[:, width:])

    @pl.when(tab_ref[2, g] == 1)
    def _():
        hfin_ref[0] = h


def _lru_scan(tab, pb, h0, cw, cb, wg, brg, big, lam, hf, *, reverse, width):
    t, nb = pb.shape
    n_steps = tab.shape[1]
    n_seq = h0.shape[0]
    nblk = width // LANES
    const = lambda g, tab: (0, 0)
    vec = pl.BlockSpec((1, width), const)
    chunk_map = lambda g, tab: (tab[0, g], 0)
    in_specs = [
        pl.BlockSpec((CHUNK, nb), chunk_map),
        pl.BlockSpec((SUBLANES, nb), _halo_prev),
        pl.BlockSpec((SUBLANES, nb), _halo_next(t // SUBLANES)),
        pl.BlockSpec((1, 1, width), lambda g, tab: (tab[3, g], 0, 0)),
        pl.BlockSpec(cw.shape, const),
        vec,
        pl.BlockSpec((nblk, LANES, 2 * LANES), lambda g, tab: (0, 0, 0)),
        vec, vec, vec,
    ]
    args = [tab, pb, pb, pb, h0, cw, cb, wg, brg, big, lam]
    scratch = [pltpu.VMEM((CHUNK, width), F32), pltpu.VMEM((CHUNK, width), F32),
               pltpu.VMEM((1, width), F32)]
    if reverse:
        in_specs.append(pl.BlockSpec((CHUNK, width), chunk_map))
        args.append(hf)
        scratch.append(pltpu.VMEM((CHUNK, width), F32))
    grid_spec = pltpu.PrefetchScalarGridSpec(
        num_scalar_prefetch=1,
        grid=(n_steps,),
        in_specs=in_specs,
        out_specs=[
            pl.BlockSpec((CHUNK, width), chunk_map),
            pl.BlockSpec((1, 1, width), lambda g, tab: (tab[3, g], 0, 0)),
        ],
        scratch_shapes=scratch,
    )
    return pl.pallas_call(
        functools.partial(_lru_kernel, reverse=reverse, width=width),
        grid_spec=grid_spec,
        out_shape=[
            jax.ShapeDtypeStruct((t, width), F32),
            jax.ShapeDtypeStruct((n_seq, 1, width), F32),
        ],
        compiler_params=_params(("arbitrary",)),
        name="lru_bwd" if reverse else "lru_fwd",
    )(*args)


def _postmix_kernel(yf_ref, yb_ref, bf_ref, bb_ref, gd_ref, hl_ref, pg_ref, x_ref, mod_ref,
                    lnw_ref, lnb_ref, gup_ref, wa_ref, wb_ref, wo_ref, bm_ref, gpost_ref,
                    gpre_ref, e_ref, et_ref, *rest, d_model, n_experts):
    if n_experts:
        router_ref, x1_ref, h2_ref, gates_ref = rest
    else:
        x1_ref, h2_ref = rest
    y = yf_ref[...] + yb_ref[...]
    inv_d = 1.0 / HEAD_DIM
    mean = _seg_sum(y, e_ref, et_ref) * inv_d
    yc = y - mean
    var = _seg_sum(yc * yc, e_ref, et_ref) * inv_d
    yn = yc * lax.rsqrt(var + LNX_EPS) * lnw_ref[...] + lnb_ref[...]
    gate = _dot(_sigmoid(gd_ref[...]).astype(BF16), gup_ref[...])
    ya = (yn + bf_ref[...] + bb_ref[...]) * gate
    proj_a = _dot(ya.astype(BF16), wa_ref[...])
    proj_b = _dot(hl_ref[...].astype(BF16), wb_ref[...])
    mg = _sigmoid(pg_ref[...] + bm_ref[...])
    m = mg[:, :d_model] * proj_a + mg[:, d_model:] * proj_b
    o = _dot(m.astype(BF16), wo_ref[...])
    x1 = x_ref[...] + mod_ref[0, 2:3, :] * _rmsnorm(o, gpost_ref[...])
    x1_ref[...] = x1
    h2 = _rmsnorm(x1, gpre_ref[...]) * (1.0 + mod_ref[0, 4:5, :]) + mod_ref[0, 3:4, :]
    h2_ref[...] = h2.astype(BF16)
    if n_experts:
        lane = lax.broadcasted_iota(jnp.int32, (1, LANES), 1).astype(F32)
        logits = jnp.where(lane < n_experts, _dot3(h2, router_ref[...]), -jnp.inf)
        m1 = jnp.max(logits, axis=-1, keepdims=True)
        i1 = jnp.min(jnp.where(logits == m1, lane, float(LANES)), axis=-1, keepdims=True)
        rest_l = jnp.where(lane == i1, -jnp.inf, logits)
        m2 = jnp.max(rest_l, axis=-1, keepdims=True)
        i2 = jnp.min(jnp.where(rest_l == m2, lane, float(LANES)), axis=-1, keepdims=True)
        e2 = jnp.exp(m2 - m1)
        w1 = 1.0 / (1.0 + e2)
        gates_ref[...] = jnp.where(lane == i1, w1, 0.0) + jnp.where(lane == i2, e2 * w1, 0.0)


def _postmix(yf, yb, bf, bb, pa, hl, pg, x, mod_t, lnw, lnb, gup, wa, wb, wo, bm, gpost, gpre,
             e, et, router, *, tm, gd_block):
    t, d = x.shape
    width = yf.shape[1]
    n_experts = 0 if router is None else router.shape[1]
    row = lambda i: (i, 0)
    const = lambda i: (0, 0)
    vec_d = pl.BlockSpec((1, d), const)
    vec_w = pl.BlockSpec((1, width), const)
    tile_w = pl.BlockSpec((tm, width), row)
    in_specs = [
        tile_w, tile_w, tile_w, tile_w,
        pl.BlockSpec((tm, LANES), lambda i: (i, gd_block)),
        tile_w,
        pl.BlockSpec((tm, 2 * d), row),
        pl.BlockSpec((tm, d), row),
        pl.BlockSpec((1, N_MOD, d), lambda i: (i, 0, 0)),
        vec_w, vec_w,
        pl.BlockSpec((LANES, width), const),
        pl.BlockSpec((width, d), const),
        pl.BlockSpec((width, d), const),
        pl.BlockSpec((d, d), const),
        pl.BlockSpec((1, 2 * d), const),
        vec_d, vec_d,
        pl.BlockSpec((width, LANES), const),
        pl.BlockSpec((LANES, width), const),
    ]
    args = [yf, yb, bf, bb, pa, hl, pg, x, mod_t, lnw, lnb, gup, wa, wb, wo, bm, gpost, gpre,
            e, et]
    out_specs = [pl.BlockSpec((tm, d), row), pl.BlockSpec((tm, d), row)]
    out_shape = [jax.ShapeDtypeStruct((t, d), F32), jax.ShapeDtypeStruct((t, d), BF16)]
    if n_experts:
        router_pad = jnp.zeros((d, LANES), F32).at[:, :n_experts].set(router)
        in_specs.append(pl.BlockSpec((d, LANES), const))
        args.append(router_pad)
        out_specs.append(pl.BlockSpec((tm, LANES), row))
        out_shape.append(jax.ShapeDtypeStruct((t, LANES), F32))
    return pl.pallas_call(
        functools.partial(_postmix_kernel, d_model=d, n_experts=n_experts),
        grid=(t // tm,),
        in_specs=in_specs,
        out_specs=out_specs,
        out_shape=out_shape,
        compiler_params=_params(("arbitrary",)),
        name="postmix",
    )(*args)


def _ffn_kernel(h_ref, x1_ref, mod_ref, gpost_ref, w1_ref, w3_ref, w2_ref, *rest, gated):
    if gated:
        gates_ref, o_ref, acc_ref = rest
    else:
        o_ref, acc_ref = rest
    e = pl.program_id(1)
    f = pl.program_id(2)

    @pl.when((e == 0) & (f == 0))
    def _():
        acc_ref[...] = jnp.zeros_like(acc_ref)

    h = h_ref[...]
    hid = _silu(_dot(h, w1_ref[...])) * _dot(h, w3_ref[...])
    if gated:
        lane = lax.broadcasted_iota(jnp.int32, (1, LANES), 1)
        hid = hid * jnp.sum(jnp.where(lane == e, gates_ref[...], 0.0), axis=-1, keepdims=True)
    acc_ref[...] += _dot(hid.astype(BF16), w2_ref[...])

    @pl.when((e == pl.num_programs(1) - 1) & (f == pl.num_programs(2) - 1))
    def _():
        o_ref[...] = x1_ref[...] + mod_ref[0, 5:6, :] * _rmsnorm(acc_ref[...], gpost_ref[...])


def _ffn(h2, x1, mod_t, gpost, w1, w3, w2, gates, *, tm, tf):
    t, d = x1.shape
    n_e, _, ff = w1.shape
    gated = gates is not None
    row = lambda i, e, f: (i, 0)
    in_specs = [
        pl.BlockSpec((tm, d), row),
        pl.BlockSpec((tm, d), row),
        pl.BlockSpec((1, N_MOD, d), lambda i, e, f: (i, 0, 0)),
        pl.BlockSpec((1, d), lambda i, e, f: (0, 0)),
        pl.BlockSpec((None, d, tf), lambda i, e, f: (e, 0, f)),
        pl.BlockSpec((None, d, tf), lambda i, e, f: (e, 0, f)),
        pl.BlockSpec((None, tf, d), lambda i, e, f: (e, f, 0)),
    ]
    args = [h2, x1, mod_t, gpost, w1, w3, w2]
    if gated:
        in_specs.append(pl.BlockSpec((tm, LANES), row))
        args.append(gates)
    return pl.pallas_call(
        functools.partial(_ffn_kernel, gated=gated),
        grid=(t // tm, n_e, ff // tf),
        in_specs=in_specs,
        out_specs=pl.BlockSpec((tm, d), row),
        out_shape=jax.ShapeDtypeStruct((t, d), F32),
        scratch_shapes=[pltpu.VMEM((tm, d), F32)],
        compiler_params=_params(("arbitrary", "arbitrary", "arbitrary")),
        name="ffn_moe" if gated else "ffn_dense",
    )(*args)


def _pair_blockdiag(s):
    b, h, d, _ = s.shape
    s = s.reshape(b, h // 2, 2, d, d)
    z = jnp.zeros((b, h // 2, d, d), s.dtype)
    top = jnp.concatenate([s[:, :, 0], z], axis=-1)
    bot = jnp.concatenate([z, s[:, :, 1]], axis=-1)
    return jnp.concatenate([top, bot], axis=-2)


def _pair_unblock(s):
    b, p, n, _ = s.shape
    d = n // 2
    s = s.reshape(b, p, 2, d, 2, d)
    return jnp.stack([s[:, :, 0, :, 0, :], s[:, :, 1, :, 1, :]], axis=2).reshape(b, 2 * p, d, d)


def _head_indicator(width):
    e = np.zeros((width, LANES), np.float32)
    e[np.arange(width), np.arange(width) // HEAD_DIM] = 1.0
    return jnp.asarray(e, BF16), jnp.asarray(e.T.copy(), BF16)


def _tile_mod(mod_l, tile_rows, tm, d):
    idx = np.asarray(tile_rows[::tm], np.int32)
    return jnp.take(mod_l, idx, axis=0).reshape(len(idx), N_MOD, d)


def _pick_tile(t, pref):
    while t % pref:
        pref //= 2
    return pref


def kernel(x_prompt, x_sample, c, c_ctx, state_rwkv, state_lru, mod_w, mod_b, norm_pre_mix, norm_post_mix, norm_pre_ffn, norm_post_ffn, w_in, b_merge, rwkv_mu, rwkv_w0, rwkv_w_up, rwkv_a0, rwkv_a_up, rwkv_k_k, rwkv_k_a, rwkv_r_k, rwkv_g_up, rwkv_lnx_w, rwkv_lnx_b, lru_conv_w, lru_conv_b, lru_w_rg, lru_b_rg, lru_w_ig, lru_b_ig, lru_lam, w_proj_a, w_proj_b, w_out, ffn_w1, ffn_w3, ffn_w2, moe_router, moe_w1, moe_w3, moe_w2):
    b_p, seq, d = x_prompt.shape
    b_s, dec_seq, _ = x_sample.shape
    depth = w_in.shape[0]
    n_dir = rwkv_mu.shape[1]
    n_heads, head_dim = rwkv_r_k.shape[1], rwkv_r_k.shape[2]
    width = n_heads * head_dim
    n_shift = rwkv_mu.shape[2]
    lru_w = lru_lam.shape[2]
    n_blk, blk = lru_w_rg.shape[2], lru_w_rg.shape[3]
    gate_lora = rwkv_g_up.shape[1]
    lora = rwkv_w_up.shape[2]
    na = n_shift + gate_lora
    nb = 2 * lru_w
    assert head_dim == HEAD_DIM and blk == LANES and n_dir == 2
    assert n_shift == 3 * width + 2 * lora and 2 * lora == LANES and gate_lora == LANES
    assert dec_seq % GRID_W == 0 and lru_conv_w.shape[1] == 4

    t_p, t_s = b_p * seq, b_s * dec_seq
    t = t_p + t_s
    x = jnp.concatenate([x_prompt.reshape(t_p, d), x_sample.reshape(t_s, d)], axis=0)

    n_cond = -(-(1 + b_s) // 16) * 16
    cond = jnp.zeros((n_cond, d), F32).at[0].set(c_ctx).at[1:1 + b_s].set(c)
    mod = _modulation(cond, mod_w, mod_b)
    tile_rows = np.concatenate([np.zeros(t_p, np.int32),
                                1 + np.arange(t_s, dtype=np.int32) // dec_seq])

    seq_lens = [seq] * b_p + [dec_seq] * b_s
    row_lens = [seq] * b_p + [GRID_W] * b_s
    tabs = [_scan_table(seq_lens, row_lens, False), _scan_table(seq_lens, row_lens, True)]
    n_seq = b_p + b_s
    e_ind, et_ind = _head_indicator(width)

    tm = _pick_tile(np.gcd(seq, dec_seq), 256)
    tm_ffn = _pick_tile(np.gcd(t_p, dec_seq), 512)

    rwkv_fin, lru_fin = [], []
    for l in range(depth):
        mod_t = _tile_mod(mod[l], tile_rows, tm, d)
        pa, pb, pg = _inproj(x, mod_t, norm_pre_mix[l][None], w_in[l].astype(BF16), na, nb, tm)

        ys, bonuses, fins = [], [], []
        for dr in range(n_dir):
            s0 = jnp.concatenate(
                [jnp.zeros((b_p, n_heads // 2, LANES, LANES), F32),
                 _pair_blockdiag(state_rwkv[:, l, dr].astype(F32))], axis=0)
            zpad = jnp.zeros((lora, width), F32)
            wup = jnp.concatenate([rwkv_w_up[l, dr], zpad], axis=0).astype(BF16)
            aup = jnp.concatenate([zpad, rwkv_a_up[l, dr]], axis=0).astype(BF16)
            y_d, bonus_d, fin_d = _rwkv_scan(
                tabs[dr], pa, s0, rwkv_mu[l, dr][None], rwkv_w0[l, dr][None], wup,
                rwkv_a0[l, dr][None], aup, rwkv_k_k[l][None], rwkv_k_a[l][None],
                rwkv_r_k[l].reshape(1, width), e_ind, et_ind,
                reverse=bool(dr), width=width, n_shift=n_shift)
            ys.append(y_d)
            bonuses.append(bonus_d)
            fins.append(_pair_unblock(fin_d[:b_p]))
        rwkv_fin.append(jnp.stack(fins, axis=1))

        hl, lfin = None, []
        for dr in range(n_dir):
            h0 = jnp.concatenate([jnp.zeros((b_p, lru_w), F32),
                                  state_lru[:, l, dr].astype(F32)], axis=0)[:, None, :]
            wg = jnp.concatenate([lru_w_rg[l, dr], lru_w_ig[l, dr]], axis=-1).astype(BF16)
            hl, fin_d = _lru_scan(
                tabs[dr], pb, h0, lru_conv_w[l], lru_conv_b[l][None], wg,
                lru_b_rg[l, dr][None], lru_b_ig[l, dr][None], lru_lam[l, dr][None], hl,
                reverse=bool(dr), width=lru_w)
            lfin.append(fin_d[:b_p, 0])
        lru_fin.append(jnp.stack(lfin, axis=1))

        is_moe = l % 2 == 1
        i = l // 2
        post = _postmix(
            ys[0], ys[1], bonuses[0], bonuses[1], pa, hl, pg, x, mod_t,
            rwkv_lnx_w[l][None], rwkv_lnx_b[l][None], rwkv_g_up[l].astype(BF16),
            w_proj_a[l].astype(BF16), w_proj_b[l].astype(BF16), w_out[l].astype(BF16),
            b_merge[l][None], norm_post_mix[l][None], norm_pre_ffn[l][None], e_ind, et_ind,
            moe_router[i] if is_moe else None, tm=tm, gd_block=n_shift // LANES)
        mod_f = _tile_mod(mod[l], tile_rows, tm_ffn, d)
        if is_moe:
            x1, h2, gates = post
            ff = moe_w1.shape[3]
            x = _ffn(h2, x1, mod_f, norm_post_ffn[l][None], moe_w1[i].astype(BF16),
                     moe_w3[i].astype(BF16), moe_w2[i].astype(BF16), gates,
                     tm=tm_ffn, tf=_pick_tile(ff, 1792) if ff % 1792 == 0 else _pick_tile(ff, 1024))
        else:
            x1, h2 = post
            ff = ffn_w1.shape[2]
            x = _ffn(h2, x1, mod_f, norm_post_ffn[l][None], ffn_w1[i][None].astype(BF16),
                     ffn_w3[i][None].astype(BF16), ffn_w2[i][None].astype(BF16), None,
                     tm=tm_ffn, tf=_pick_tile(ff, 1024))

    y_prompt = x[:t_p].reshape(b_p, seq, d).astype(x_prompt.dtype)
    y_sample = x[t_p:].reshape(b_s, dec_seq, d).astype(x_sample.dtype)
    new_state_rwkv = jnp.stack(rwkv_fin, axis=1).astype(x_prompt.dtype)
    new_state_lru = jnp.stack(lru_fin, axis=1).astype(x_prompt.dtype)
    return (y_prompt, y_sample, new_state_rwkv, new_state_lru)
```

```python
import functools

import numpy as np
import jax
import jax.numpy as jnp
from jax import lax
from jax.experimental import pallas as pl
from jax.experimental.pallas import tpu as pltpu

F32 = jnp.float32
BF16 = jnp.bfloat16

GRID_W = 64
CHUNK = 64
LANES = 128
SUBLANES = 8
HEAD_DIM = 64
GROUP = 256
LRU_C = 8.0
EPS = 1e-6
LNX_EPS = 64e-5
N_MOD = 6
VMEM_LIMIT = 56 * 1024 * 1024


def _dot(a, b):
    return jnp.dot(a, b, preferred_element_type=F32)


def _dot_nt(a, b):
    return lax.dot_general(a, b, (((1,), (1,)), ((), ())), preferred_element_type=F32)


def _dot_tn(a, b):
    return lax.dot_general(a, b, (((0,), (0,)), ((), ())), preferred_element_type=F32)


def _split2(x):
    hi = x.astype(BF16)
    lo = (x - hi.astype(F32)).astype(BF16)
    return hi, lo


def _dot3(a, b):
    ah, al = _split2(a)
    bh, bl = _split2(b)
    return _dot(ah, bh) + _dot(ah, bl) + _dot(al, bh)


def _sigmoid(x):
    return 1.0 / (1.0 + jnp.exp(-x))


def _softplus(x):
    return jnp.maximum(x, 0.0) + jnp.log(1.0 + jnp.exp(-jnp.abs(x)))


def _silu(x):
    return x * _sigmoid(x)


def _gelu_tanh(x):
    return 0.5 * x * (1.0 + jnp.tanh(0.7978845608028654 * (x + 0.044715 * (x * x * x))))


def _rmsnorm(x, g):
    ms = jnp.mean(x * x, axis=-1, keepdims=True)
    return x * lax.rsqrt(ms + EPS) * g


def _seg_sum(x, e_ref, et_ref):
    hi, lo = _split2(x)
    hs = _dot(hi, e_ref[...]) + _dot(lo, e_ref[...])
    hh, hl = _split2(hs)
    return _dot(hh, et_ref[...]) + _dot(hl, et_ref[...])


def _params(sem):
    return pltpu.CompilerParams(dimension_semantics=sem, vmem_limit_bytes=VMEM_LIMIT)


def _mod_kernel(c_ref, w_ref, b_ref, o_ref):
    s = _silu(c_ref[...])
    o_ref[...] = _dot3(s, w_ref[...]) + b_ref[...]


def _modulation(cond, mod_w, mod_b):
    nl, d, n = mod_w.shape
    rows = cond.shape[0]
    tn = d
    return pl.pallas_call(
        _mod_kernel,
        grid=(nl, n // tn),
        in_specs=[
            pl.BlockSpec((rows, d), lambda l, j: (0, 0)),
            pl.BlockSpec((None, d, tn), lambda l, j: (l, 0, j)),
            pl.BlockSpec((None, 1, tn), lambda l, j: (l, 0, j)),
        ],
        out_specs=pl.BlockSpec((None, rows, tn), lambda l, j: (l, 0, j)),
        out_shape=jax.ShapeDtypeStruct((nl, rows, n), F32),
        compiler_params=_params(("arbitrary", "arbitrary")),
        name="modulation",
    )(cond, mod_w, mod_b.reshape(nl, 1, n))


def _inproj_kernel(x_ref, mod_ref, g_ref, w_ref, pa_ref, pb_ref, pg_ref, *, na, nb):
    h = _rmsnorm(x_ref[...], g_ref[...])
    h = h * (1.0 + mod_ref[0, 1:2, :]) + mod_ref[0, 0:1, :]
    hb = h.astype(BF16)
    pa_ref[...] = _dot(hb, w_ref[:, :na])
    pb_ref[...] = _dot(hb, w_ref[:, na:na + nb])
    pg_ref[...] = _dot(hb, w_ref[:, na + nb:])


def _inproj(x, mod_t, g, w, na, nb, tm):
    t, d = x.shape
    n = w.shape[1]
    ng = n - na - nb
    return pl.pallas_call(
        functools.partial(_inproj_kernel, na=na, nb=nb),
        grid=(t // tm,),
        in_specs=[
            pl.BlockSpec((tm, d), lambda i: (i, 0)),
            pl.BlockSpec((1, N_MOD, d), lambda i: (i, 0, 0)),
            pl.BlockSpec((1, d), lambda i: (0, 0)),
            pl.BlockSpec((d, n), lambda i: (0, 0)),
        ],
        out_specs=[
            pl.BlockSpec((tm, na), lambda i: (i, 0)),
            pl.BlockSpec((tm, nb), lambda i: (i, 0)),
            pl.BlockSpec((tm, ng), lambda i: (i, 0)),
        ],
        out_shape=[
            jax.ShapeDtypeStruct((t, na), F32),
            jax.ShapeDtypeStruct((t, nb), F32),
            jax.ShapeDtypeStruct((t, ng), F32),
        ],
        compiler_params=_params(("arbitrary",)),
        name="inproj",
    )(x, mod_t, g, w)


def _scan_table(seq_lens, row_lens, reverse):
    rows = [[] for _ in range(6)]
    base = 0
    for s, (ls, rl) in enumerate(zip(seq_lens, row_lens)):
        assert ls % CHUNK == 0 and rl % CHUNK == 0 and ls % rl == 0
        nc = ls // CHUNK
        order = range(nc - 1, -1, -1) if reverse else range(nc)
        for n, ci in enumerate(order):
            t0 = ci * CHUNK
            rows[0].append(base + ci)
            rows[1].append(int(n == 0))
            rows[2].append(int(n == nc - 1))
            rows[3].append(s)
            rows[4].append(int(t0 % rl != 0))
            rows[5].append(int((t0 + CHUNK) % rl != 0))
        base += nc
    return jnp.asarray(np.array(rows, dtype=np.int32))


def _halo_prev(g, tab):
    return (jnp.maximum(tab[0, g] * (CHUNK // SUBLANES) - 1, 0), 0)


def _halo_next(n_blocks8):
    def index_map(g, tab):
        return (jnp.minimum((tab[0, g] + 1) * (CHUNK // SUBLANES), n_blocks8 - 1), 0)
    return index_map


def _rwkv_kernel(tab_ref, pa_ref, halo_ref, s0_ref, mu_ref, w0_ref, wup_ref, a0_ref, aup_ref,
                 kk_ref, ka_ref, rk_ref, e_ref, et_ref,
                 y_ref, bonus_ref, sfin_ref, s_scr, *, reverse, width, n_shift):
    g = pl.program_id(0)

    @pl.when(tab_ref[1, g] == 1)
    def _():
        s_scr[...] = s0_ref[0]

    row = lax.broadcasted_iota(jnp.int32, (CHUNK, 1), 0)
    if reverse:
        edge = row == CHUNK - 1
        halo_ok = tab_ref[5, g].astype(F32)
        halo_row, shift = 0, CHUNK - 1
    else:
        edge = row == 0
        halo_ok = tab_ref[4, g].astype(F32)
        halo_row, shift = SUBLANES - 1, 1

    def mix(lo, hi):
        ps = pa_ref[:, lo:hi]
        nb = halo_ref[halo_row:halo_row + 1, lo:hi] * halo_ok
        shifted = jnp.where(edge, nb, pltpu.roll(ps, shift, 0))
        return ps + (shifted - ps) * mu_ref[:, lo:hi]

    r = mix(0, width)
    k = mix(width, 2 * width)
    v = mix(2 * width, 3 * width)
    wa = mix(3 * width, n_shift)

    lw = w0_ref[...] + _dot(jnp.tanh(wa).astype(BF16), wup_ref[...])
    logw = -jnp.exp(-_softplus(-lw) - 0.5)
    a = _sigmoid(a0_ref[...] + _dot(wa.astype(BF16), aup_ref[...]))

    kk = k * kk_ref[...]
    nrm = jnp.sqrt(_seg_sum(kk * kk, e_ref, et_ref))
    kk = kk / jnp.maximum(nrm, 1e-12)
    k = k * (1.0 + (a - 1.0) * ka_ref[...])
    bonus_ref[...] = _seg_sum(r * k * rk_ref[...], e_ref, et_ref) * v

    ti = lax.broadcasted_iota(jnp.int32, (CHUNK, CHUNK), 0)
    si = lax.broadcasted_iota(jnp.int32, (CHUNK, CHUNK), 1)
    tri = jnp.where((si >= ti) if reverse else (si <= ti), 1.0, 0.0).astype(BF16)
    l1 = logw.astype(BF16)
    rem = logw - l1.astype(F32)
    l2 = rem.astype(BF16)
    l3 = (rem - l2.astype(F32)).astype(BF16)
    gc = _dot(tri, l1) + _dot(tri, l2) + _dot(tri, l3)
    g_end = gc[0:1, :] if reverse else gc[CHUNK - 1:CHUNK, :]

    beta = kk * a
    inv = jnp.exp(-gc)
    to_end = jnp.exp(g_end - gc)
    r_t = r * jnp.exp(gc)
    a_t = -kk * jnp.exp(gc - logw)
    b_t = beta * inv
    k_t = k * inv
    b_h = beta * to_end
    k_h = k * to_end
    dec_end = jnp.exp(g_end)

    lane_head = lax.broadcasted_iota(jnp.int32, (1, GROUP), 1) // HEAD_DIM
    head_masks = [lane_head == h for h in range(GROUP // HEAD_DIM)]

    def bd(x):
        return jnp.concatenate([jnp.where(m, x, 0.0) for m in head_masks], axis=0).astype(BF16)

    t_row = lax.broadcasted_iota(jnp.int32, (CHUNK, GROUP), 0)
    s_lane = lax.broadcasted_iota(jnp.int32, (CHUNK, GROUP), 1) % CHUNK
    strict = (s_lane > t_row) if reverse else (s_lane < t_row)
    incl = (s_lane >= t_row) if reverse else (s_lane <= t_row)
    eye = jnp.where(s_lane == t_row, 1.0, 0.0)
    same_head = (lax.broadcasted_iota(jnp.int32, (GROUP, GROUP), 0) // HEAD_DIM
                 == lax.broadcasted_iota(jnp.int32, (GROUP, GROUP), 1) // HEAD_DIM)

    groups = range(width // GROUP)
    gsl = [slice(q * GROUP, (q + 1) * GROUP) for q in groups]
    cat = jnp.concatenate

    lhs_ar = [cat([a_t[:, s], r_t[:, s]], axis=0).astype(BF16) for s in gsl]
    v_bd = [bd(v[:, s]) for s in gsl]
    aa = [_dot_nt(lhs_ar[q], cat([bd(b_t[:, s]), bd(k_t[:, s])], axis=0))
          for q, s in enumerate(gsl)]
    a_ab = [jnp.where(strict, x[:CHUNK, :GROUP], 0.0) for x in aa]
    a_ak = [jnp.where(strict, x[:CHUNK, GROUP:], 0.0).astype(BF16) for x in aa]
    a_r = [cat([jnp.where(incl, x[CHUNK:, :GROUP], 0.0), jnp.where(incl, x[CHUNK:, GROUP:], 0.0)],
               axis=1).astype(BF16) for x in aa]
    s_old = [s_scr[q] for q in groups]
    s_lhs = [_dot_nt(lhs_ar[q], s_old[q].astype(BF16)) for q in groups]
    rhs = [s_lhs[q][:CHUNK] + _dot(a_ak[q], v_bd[q]) for q in groups]

    inv_m = [eye + x for x in a_ab]
    pw = [_dot(x.astype(BF16), bd(x)) for x in a_ab]
    for _ in range(CHUNK.bit_length() - 3):
        both = [_dot(cat([pw[q], inv_m[q]], axis=0).astype(BF16), bd(pw[q])) for q in groups]
        pw = [x[:CHUNK] for x in both]
        inv_m = [inv_m[q] + both[q][CHUNK:] for q in groups]
    inv_m = [inv_m[q] + _dot(inv_m[q].astype(BF16), bd(pw[q])) for q in groups]

    u = [_dot(inv_m[q].astype(BF16), bd(rhs[q])) for q in groups]
    for q, s in enumerate(gsl):
        y_ref[:, s] = s_lhs[q][CHUNK:] + _dot(a_r[q], cat([bd(u[q]), v_bd[q]], axis=0))
    for q, s in enumerate(gsl):
        upd = _dot_tn(cat([u[q], v[:, s]], axis=0).astype(BF16),
                      cat([b_h[:, s], k_h[:, s]], axis=0).astype(BF16))
        s_scr[q] = s_old[q] * dec_end[:, s] + jnp.where(same_head, upd, 0.0)

    @pl.when(tab_ref[2, g] == 1)
    def _():
        sfin_ref[0] = s_scr[...]


def _rwkv_scan(tab, pa, s0, mu, w0, wup, a0, aup, k_k, k_a, r_k, e, et, *, reverse, width,
               n_shift):
    t, na = pa.shape
    n_steps = tab.shape[1]
    n_seq, n_groups = s0.shape[0], s0.shape[1]
    state_spec = pl.BlockSpec((1, n_groups, GROUP, GROUP), lambda g, tab: (tab[3, g], 0, 0, 0))
    const = lambda g, tab: (0, 0)
    vec = pl.BlockSpec((1, width), const)
    halo_map = _halo_next(t // SUBLANES) if reverse else _halo_prev
    grid_spec = pltpu.PrefetchScalarGridSpec(
        num_scalar_prefetch=1,
        grid=(n_steps,),
        in_specs=[
            pl.BlockSpec((CHUNK, na), lambda g, tab: (tab[0, g], 0)),
            pl.BlockSpec((SUBLANES, na), halo_map),
            state_spec,
            pl.BlockSpec((1, n_shift), const),
            vec,
            pl.BlockSpec((LANES, width), const),
            vec,
            pl.BlockSpec((LANES, width), const),
            vec, vec, vec,
            pl.BlockSpec((width, LANES), const),
            pl.BlockSpec((LANES, width), const),
        ],
        out_specs=[
            pl.BlockSpec((CHUNK, width), lambda g, tab: (tab[0, g], 0)),
            pl.BlockSpec((CHUNK, width), lambda g, tab: (tab[0, g], 0)),
            state_spec,
        ],
        scratch_shapes=[pltpu.VMEM((n_groups, GROUP, GROUP), F32)],
    )
    return pl.pallas_call(
        functools.partial(_rwkv_kernel, reverse=reverse, width=width, n_shift=n_shift),
        grid_spec=grid_spec,
        out_shape=[
            jax.ShapeDtypeStruct((t, width), F32),
            jax.ShapeDtypeStruct((t, width), F32),
            jax.ShapeDtypeStruct((n_seq, n_groups, GROUP, GROUP), F32),
        ],
        compiler_params=_params(("arbitrary",)),
        name="rwkv_bwd" if reverse else "rwkv_fwd",
    )(tab, pa, pa, s0, mu, w0, wup, a0, aup, k_k, k_a, r_k, e, et)


def _lru_kernel(tab_ref, pb_ref, hp_ref, hn_ref, h0_ref, cw_ref, cb_ref, wg_ref, brg_ref,
                big_ref, lam_ref, *rest, reverse, width):
    if reverse:
        hf_ref, out_ref, hfin_ref, a_scr, u_scr, h_scr, hb_scr = rest
        dst = hb_scr
    else:
        out_ref, hfin_ref, a_scr, u_scr, h_scr = rest
        dst = out_ref
    g = pl.program_id(0)

    @pl.when(tab_ref[1, g] == 1)
    def _():
        h_scr[...] = h0_ref[0]

    prev_ok = tab_ref[4, g].astype(F32)
    next_ok = tab_ref[5, g].astype(F32)
    row = lax.broadcasted_iota(jnp.int32, (CHUNK, 1), 0)
    sp = _softplus(-lam_ref[...])

    for n in range(width // LANES):
        sl = slice(n * LANES, (n + 1) * LANES)
        x = pb_ref[:, sl]
        p7 = hp_ref[SUBLANES - 1:SUBLANES, sl] * prev_ok
        p6 = hp_ref[SUBLANES - 2:SUBLANES - 1, sl] * prev_ok
        n0 = hn_ref[0:1, sl] * next_ok
        xm1 = jnp.where(row == 0, p7, pltpu.roll(x, 1, 0))
        xm2 = jnp.where(row == 0, p6, jnp.where(row == 1, p7, pltpu.roll(x, 2, 0)))
        xp1 = jnp.where(row == CHUNK - 1, n0, pltpu.roll(x, CHUNK - 1, 0))
        xc = (xm2 * cw_ref[0:1, sl] + xm1 * cw_ref[1:2, sl] + x * cw_ref[2:3, sl]
              + xp1 * cw_ref[3:4, sl] + cb_ref[:, sl])
        gz = _dot(xc.astype(BF16), wg_ref[n])
        rg = _sigmoid(gz[:, :LANES] + brg_ref[:, sl])
        ig = _sigmoid(gz[:, LANES:] + big_ref[:, sl])
        log_a = (-LRU_C) * rg * sp[:, sl]
        a_scr[:, sl] = jnp.exp(log_a)
        u_scr[:, sl] = jnp.sqrt(1.0 - jnp.exp(2.0 * log_a)) * ig * xc

    h = h_scr[...]
    for t in (range(CHUNK - 1, -1, -1) if reverse else range(CHUNK)):
        h = a_scr[t:t + 1, :] * h + u_scr[t:t + 1, :]
        dst[t:t + 1, :] = h
    h_scr[...] = h

    if reverse:
        out_ref[...] = (hf_ref[...] + hb_scr[...]) * _gelu_tanh(pb_ref[:, width:])

    @pl.when(tab_ref[2, g] == 1)
    def _():
        hfin_ref[0] = h


def _lru_scan(tab, pb, h0, cw, cb, wg, brg, big, lam, hf, *, reverse, width):
    t, nb = pb.shape
    n_steps = tab.shape[1]
    n_seq = h0.shape[0]
    nblk = width // LANES
    const = lambda g, tab: (0, 0)
    vec = pl.BlockSpec((1, width), const)
    chunk_map = lambda g, tab: (tab[0, g], 0)
    in_specs = [
        pl.BlockSpec((CHUNK, nb), chunk_map),
        pl.BlockSpec((SUBLANES, nb), _halo_prev),
        pl.BlockSpec((SUBLANES, nb), _halo_next(t // SUBLANES)),
        pl.BlockSpec((1, 1, width), lambda g, tab: (tab[3, g], 0, 0)),
        pl.BlockSpec(cw.shape, const),
        vec,
        pl.BlockSpec((nblk, LANES, 2 * LANES), lambda g, tab: (0, 0, 0)),
        vec, vec, vec,
    ]
    args = [tab, pb, pb, pb, h0, cw, cb, wg, brg, big, lam]
    scratch = [pltpu.VMEM((CHUNK, width), F32), pltpu.VMEM((CHUNK, width), F32),
               pltpu.VMEM((1, width), F32)]
    if reverse:
        in_specs.append(pl.BlockSpec((CHUNK, width), chunk_map))
        args.append(hf)
        scratch.append(pltpu.VMEM((CHUNK, width), F32))
    grid_spec = pltpu.PrefetchScalarGridSpec(
        num_scalar_prefetch=1,
        grid=(n_steps,),
        in_specs=in_specs,
        out_specs=[
            pl.BlockSpec((CHUNK, width), chunk_map),
            pl.BlockSpec((1, 1, width), lambda g, tab: (tab[3, g], 0, 0)),
        ],
        scratch_shapes=scratch,
    )
    return pl.pallas_call(
        functools.partial(_lru_kernel, reverse=reverse, width=width),
        grid_spec=grid_spec,
        out_shape=[
            jax.ShapeDtypeStruct((t, width), F32),
            jax.ShapeDtypeStruct((n_seq, 1, width), F32),
        ],
        compiler_params=_params(("arbitrary",)),
        name="lru_bwd" if reverse else "lru_fwd",
    )(*args)


def _postmix_kernel(yf_ref, yb_ref, bf_ref, bb_ref, gd_ref, hl_ref, pg_ref, x_ref, mod_ref,
                    lnw_ref, lnb_ref, gup_ref, wa_ref, wb_ref, wo_ref, bm_ref, gpost_ref,
                    gpre_ref, e_ref, et_ref, *rest, d_model, n_experts):
    if n_experts:
        router_ref, x1_ref, h2_ref, gates_ref = rest
    else:
        x1_ref, h2_ref = rest
    y = yf_ref[...] + yb_ref[...]
    inv_d = 1.0 / HEAD_DIM
    mean = _seg_sum(y, e_ref, et_ref) * inv_d
    yc = y - mean
    var = _seg_sum(yc * yc, e_ref, et_ref) * inv_d
    yn = yc * lax.rsqrt(var + LNX_EPS) * lnw_ref[...] + lnb_ref[...]
    gate = _dot(_sigmoid(gd_ref[...]).astype(BF16), gup_ref[...])
    ya = (yn + bf_ref[...] + bb_ref[...]) * gate
    proj_a = _dot(ya.astype(BF16), wa_ref[...])
    proj_b = _dot(hl_ref[...].astype(BF16), wb_ref[...])
    mg = _sigmoid(pg_ref[...] + bm_ref[...])
    m = mg[:, :d_model] * proj_a + mg[:, d_model:] * proj_b
    o = _dot(m.astype(BF16), wo_ref[...])
    x1 = x_ref[...] + mod_ref[0, 2:3, :] * _rmsnorm(o, gpost_ref[...])
    x1_ref[...] = x1
    h2 = _rmsnorm(x1, gpre_ref[...]) * (1.0 + mod_ref[0, 4:5, :]) + mod_ref[0, 3:4, :]
    h2_ref[...] = h2.astype(BF16)
    if n_experts:
        lane = lax.broadcasted_iota(jnp.int32, (1, LANES), 1).astype(F32)
        logits = jnp.where(lane < n_experts, _dot3(h2, router_ref[...]), -jnp.inf)
        m1 = jnp.max(logits, axis=-1, keepdims=True)
        i1 = jnp.min(jnp.where(logits == m1, lane, float(LANES)), axis=-1, keepdims=True)
        rest_l = jnp.where(lane == i1, -jnp.inf, logits)
        m2 = jnp.max(rest_l, axis=-1, keepdims=True)
        i2 = jnp.min(jnp.where(rest_l == m2, lane, float(LANES)), axis=-1, keepdims=True)
        e2 = jnp.exp(m2 - m1)
        w1 = 1.0 / (1.0 + e2)
        gates_ref[...] = jnp.where(lane == i1, w1, 0.0) + jnp.where(lane == i2, e2 * w1, 0.0)


def _postmix(yf, yb, bf, bb, pa, hl, pg, x, mod_t, lnw, lnb, gup, wa, wb, wo, bm, gpost, gpre,
             e, et, router, *, tm, gd_block):
    t, d = x.shape
    width = yf.shape[1]
    n_experts = 0 if router is None else router.shape[1]
    row = lambda i: (i, 0)
    const = lambda i: (0, 0)
    vec_d = pl.BlockSpec((1, d), const)
    vec_w = pl.BlockSpec((1, width), const)
    tile_w = pl.BlockSpec((tm, width), row)
    in_specs = [
        tile_w, tile_w, tile_w, tile_w,
        pl.BlockSpec((tm, LANES), lambda i: (i, gd_block)),
        tile_w,
        pl.BlockSpec((tm, 2 * d), row),
        pl.BlockSpec((tm, d), row),
        pl.BlockSpec((1, N_MOD, d), lambda i: (i, 0, 0)),
        vec_w, vec_w,
        pl.BlockSpec((LANES, width), const),
        pl.BlockSpec((width, d), const),
        pl.BlockSpec((width, d), const),
        pl.BlockSpec((d, d), const),
        pl.BlockSpec((1, 2 * d), const),
        vec_d, vec_d,
        pl.BlockSpec((width, LANES), const),
        pl.BlockSpec((LANES, width), const),
    ]
    args = [yf, yb, bf, bb, pa, hl, pg, x, mod_t, lnw, lnb, gup, wa, wb, wo, bm, gpost, gpre,
            e, et]
    out_specs = [pl.BlockSpec((tm, d), row), pl.BlockSpec((tm, d), row)]
    out_shape = [jax.ShapeDtypeStruct((t, d), F32), jax.ShapeDtypeStruct((t, d), BF16)]
    if n_experts:
        router_pad = jnp.zeros((d, LANES), F32).at[:, :n_experts].set(router)
        in_specs.append(pl.BlockSpec((d, LANES), const))
        args.append(router_pad)
        out_specs.append(pl.BlockSpec((tm, LANES), row))
        out_shape.append(jax.ShapeDtypeStruct((t, LANES), F32))
    return pl.pallas_call(
        functools.partial(_postmix_kernel, d_model=d, n_experts=n_experts),
        grid=(t // tm,),
        in_specs=in_specs,
        out_specs=out_specs,
        out_shape=out_shape,
        compiler_params=_params(("arbitrary",)),
        name="postmix",
    )(*args)


def _ffn_kernel(h_ref, x1_ref, mod_ref, gpost_ref, w1_ref, w3_ref, w2_ref, *rest, gated):
    if gated:
        gates_ref, o_ref, acc_ref = rest
    else:
        o_ref, acc_ref = rest
    e = pl.program_id(1)
    f = pl.program_id(2)

    @pl.when((e == 0) & (f == 0))
    def _():
        acc_ref[...] = jnp.zeros_like(acc_ref)

    h = h_ref[...]
    hid = _silu(_dot(h, w1_ref[...])) * _dot(h, w3_ref[...])
    if gated:
        lane = lax.broadcasted_iota(jnp.int32, (1, LANES), 1)
        hid = hid * jnp.sum(jnp.where(lane == e, gates_ref[...], 0.0), axis=-1, keepdims=True)
    acc_ref[...] += _dot(hid.astype(BF16), w2_ref[...])

    @pl.when((e == pl.num_programs(1) - 1) & (f == pl.num_programs(2) - 1))
    def _():
        o_ref[...] = x1_ref[...] + mod_ref[0, 5:6, :] * _rmsnorm(acc_ref[...], gpost_ref[...])


def _ffn(h2, x1, mod_t, gpost, w1, w3, w2, gates, *, tm, tf):
    t, d = x1.shape
    n_e, _, ff = w1.shape
    gated = gates is not None
    row = lambda i, e, f: (i, 0)
    in_specs = [
        pl.BlockSpec((tm, d), row),
        pl.BlockSpec((tm, d), row),
        pl.BlockSpec((1, N_MOD, d), lambda i, e, f: (i, 0, 0)),
        pl.BlockSpec((1, d), lambda i, e, f: (0, 0)),
        pl.BlockSpec((None, d, tf), lambda i, e, f: (e, 0, f)),
        pl.BlockSpec((None, d, tf), lambda i, e, f: (e, 0, f)),
        pl.BlockSpec((None, tf, d), lambda i, e, f: (e, f, 0)),
    ]
    args = [h2, x1, mod_t, gpost, w1, w3, w2]
    if gated:
        in_specs.append(pl.BlockSpec((tm, LANES), row))
        args.append(gates)
    return pl.pallas_call(
        functools.partial(_ffn_kernel, gated=gated),
        grid=(t // tm, n_e, ff // tf),
        in_specs=in_specs,
        out_specs=pl.BlockSpec((tm, d), row),
        out_shape=jax.ShapeDtypeStruct((t, d), F32),
        scratch_shapes=[pltpu.VMEM((tm, d), F32)],
        compiler_params=_params(("arbitrary", "arbitrary", "arbitrary")),
        name="ffn_moe" if gated else "ffn_dense",
    )(*args)


def _group_blockdiag(s):
    b, h, d, _ = s.shape
    n = GROUP // d
    s = s.reshape(b, h // n, n, d, d)
    on_diag = jnp.eye(n, dtype=bool)[None, None, :, None, :, None]
    return jnp.where(on_diag, s[:, :, :, :, None, :], 0).reshape(b, h // n, n * d, n * d)


def _group_unblock(s):
    b, q, _, _ = s.shape
    n = GROUP // HEAD_DIM
    s = s.reshape(b, q, n, HEAD_DIM, n, HEAD_DIM)
    return jnp.stack([s[:, :, h, :, h, :] for h in range(n)], axis=2).reshape(
        b, q * n, HEAD_DIM, HEAD_DIM)


def _head_indicator(width):
    e = np.zeros((width, LANES), np.float32)
    e[np.arange(width), np.arange(width) // HEAD_DIM] = 1.0
    return jnp.asarray(e, BF16), jnp.asarray(e.T.copy(), BF16)


def _tile_mod(mod_l, tile_rows, tm, d):
    idx = np.asarray(tile_rows[::tm], np.int32)
    return jnp.take(mod_l, idx, axis=0).reshape(len(idx), N_MOD, d)


def _pick_tile(t, pref):
    while t % pref:
        pref //= 2
    return pref


def kernel(x_prompt, x_sample, c, c_ctx, state_rwkv, state_lru, mod_w, mod_b, norm_pre_mix, norm_post_mix, norm_pre_ffn, norm_post_ffn, w_in, b_merge, rwkv_mu, rwkv_w0, rwkv_w_up, rwkv_a0, rwkv_a_up, rwkv_k_k, rwkv_k_a, rwkv_r_k, rwkv_g_up, rwkv_lnx_w, rwkv_lnx_b, lru_conv_w, lru_conv_b, lru_w_rg, lru_b_rg, lru_w_ig, lru_b_ig, lru_lam, w_proj_a, w_proj_b, w_out, ffn_w1, ffn_w3, ffn_w2, moe_router, moe_w1, moe_w3, moe_w2):
    b_p, seq, d = x_prompt.shape
    b_s, dec_seq, _ = x_sample.shape
    depth = w_in.shape[0]
    n_dir = rwkv_mu.shape[1]
    n_heads, head_dim = rwkv_r_k.shape[1], rwkv_r_k.shape[2]
    width = n_heads * head_dim
    n_shift = rwkv_mu.shape[2]
    lru_w = lru_lam.shape[2]
    n_blk, blk = lru_w_rg.shape[2], lru_w_rg.shape[3]
    gate_lora = rwkv_g_up.shape[1]
    lora = rwkv_w_up.shape[2]
    na = n_shift + gate_lora
    nb = 2 * lru_w
    assert head_dim == HEAD_DIM and blk == LANES and n_dir == 2
    assert n_shift == 3 * width + 2 * lora and 2 * lora == LANES and gate_lora == LANES
    assert dec_seq % GRID_W == 0 and lru_conv_w.shape[1] == 4

    t_p, t_s = b_p * seq, b_s * dec_seq
    t = t_p + t_s
    x = jnp.concatenate([x_prompt.reshape(t_p, d), x_sample.reshape(t_s, d)], axis=0)

    n_cond = -(-(1 + b_s) // 16) * 16
    cond = jnp.zeros((n_cond, d), F32).at[0].set(c_ctx).at[1:1 + b_s].set(c)
    mod = _modulation(cond, mod_w, mod_b)
    tile_rows = np.concatenate([np.zeros(t_p, np.int32),
                                1 + np.arange(t_s, dtype=np.int32) // dec_seq])

    seq_lens = [seq] * b_p + [dec_seq] * b_s
    row_lens = [seq] * b_p + [GRID_W] * b_s
    tabs = [_scan_table(seq_lens, row_lens, False), _scan_table(seq_lens, row_lens, True)]
    n_seq = b_p + b_s
    e_ind, et_ind = _head_indicator(width)

    tm = _pick_tile(np.gcd(seq, dec_seq), 256)
    tm_ffn = _pick_tile(np.gcd(t_p, dec_seq), 512)

    rwkv_fin, lru_fin = [], []
    for l in range(depth):
        mod_t = _tile_mod(mod[l], tile_rows, tm, d)
        pa, pb, pg = _inproj(x, mod_t, norm_pre_mix[l][None], w_in[l].astype(BF16), na, nb, tm)

        ys, bonuses, fins = [], [], []
        for dr in range(n_dir):
            s0 = jnp.concatenate(
                [jnp.zeros((b_p, width // GROUP, GROUP, GROUP), F32),
                 _group_blockdiag(state_rwkv[:, l, dr].astype(F32))], axis=0)
            zpad = jnp.zeros((lora, width), F32)
            wup = jnp.concatenate([rwkv_w_up[l, dr], zpad], axis=0).astype(BF16)
            aup = jnp.concatenate([zpad, rwkv_a_up[l, dr]], axis=0).astype(BF16)
            y_d, bonus_d, fin_d = _rwkv_scan(
                tabs[dr], pa, s0, rwkv_mu[l, dr][None], rwkv_w0[l, dr][None], wup,
                rwkv_a0[l, dr][None], aup, rwkv_k_k[l][None], rwkv_k_a[l][None],
                rwkv_r_k[l].reshape(1, width), e_ind, et_ind,
                reverse=bool(dr), width=width, n_shift=n_shift)
            ys.append(y_d)
            bonuses.append(bonus_d)
            fins.append(_group_unblock(fin_d[:b_p]))
        rwkv_fin.append(jnp.stack(fins, axis=1))

        hl, lfin = None, []
        for dr in range(n_dir):
            h0 = jnp.concatenate([jnp.zeros((b_p, lru_w), F32),
                                  state_lru[:, l, dr].astype(F32)], axis=0)[:, None, :]
            wg = jnp.concatenate([lru_w_rg[l, dr], lru_w_ig[l, dr]], axis=-1).astype(BF16)
            hl, fin_d = _lru_scan(
                tabs[dr], pb, h0, lru_conv_w[l], lru_conv_b[l][None], wg,
                lru_b_rg[l, dr][None], lru_b_ig[l, dr][None], lru_lam[l, dr][None], hl,
                reverse=bool(dr), width=lru_w)
            lfin.append(fin_d[:b_p, 0])
        lru_fin.append(jnp.stack(lfin, axis=1))

        is_moe = l % 2 == 1
        i = l // 2
        post = _postmix(
            ys[0], ys[1], bonuses[0], bonuses[1], pa, hl, pg, x, mod_t,
            rwkv_lnx_w[l][None], rwkv_lnx_b[l][None], rwkv_g_up[l].astype(BF16),
            w_proj_a[l].astype(BF16), w_proj_b[l].astype(BF16), w_out[l].astype(BF16),
            b_merge[l][None], norm_post_mix[l][None], norm_pre_ffn[l][None], e_ind, et_ind,
            moe_router[i] if is_moe else None, tm=tm, gd_block=n_shift // LANES)
        mod_f = _tile_mod(mod[l], tile_rows, tm_ffn, d)
        if is_moe:
            x1, h2, gates = post
            ff = moe_w1.shape[3]
            x = _ffn(h2, x1, mod_f, norm_post_ffn[l][None], moe_w1[i].astype(BF16),
                     moe_w3[i].astype(BF16), moe_w2[i].astype(BF16), gates,
                     tm=tm_ffn, tf=_pick_tile(ff, 1792) if ff % 1792 == 0 else _pick_tile(ff, 1024))
        else:
            x1, h2 = post
            ff = ffn_w1.shape[2]
            x = _ffn(h2, x1, mod_f, norm_post_ffn[l][None], ffn_w1[i][None].astype(BF16),
                     ffn_w3[i][None].astype(BF16), ffn_w2[i][None].astype(BF16), None,
                     tm=tm_ffn, tf=_pick_tile(ff, 1024))

    y_prompt = x[:t_p].reshape(b_p, seq, d).astype(x_prompt.dtype)
    y_sample = x[t_p:].reshape(b_s, dec_seq, d).astype(x_sample.dtype)
    new_state_rwkv = jnp.stack(rwkv_fin, axis=1).astype(x_prompt.dtype)
    new_state_lru = jnp.stack(lru_fin, axis=1).astype(x_prompt.dtype)
    return (y_prompt, y_sample, new_state_rwkv, new_state_lru)
```

```python
import functools

import numpy as np
import jax
import jax.numpy as jnp
from jax import lax
from jax.experimental import pallas as pl
from jax.experimental.pallas import tpu as pltpu

F32 = jnp.float32
BF16 = jnp.bfloat16

GRID_W = 64
CHUNK = 64
LANES = 128
SUBLANES = 8
HEAD_DIM = 64
GROUP = 256
MOE_SUB = 128
LRU_C = 8.0
EPS = 1e-6
LNX_EPS = 64e-5
N_MOD = 6
N_DIR = 2
VMEM_LIMIT = 56 * 1024 * 1024


def _dot(a, b):
    return jnp.dot(a, b, preferred_element_type=F32)


def _dot_nt(a, b):
    return lax.dot_general(a, b, (((1,), (1,)), ((), ())), preferred_element_type=F32)


def _dot_tn(a, b):
    return lax.dot_general(a, b, (((0,), (0,)), ((), ())), preferred_element_type=F32)


def _split2(x):
    hi = x.astype(BF16)
    lo = (x - hi.astype(F32)).astype(BF16)
    return hi, lo


def _split3(x):
    p1 = x.astype(BF16)
    rem = x - p1.astype(F32)
    p2 = rem.astype(BF16)
    p3 = (rem - p2.astype(F32)).astype(BF16)
    return p1, p2, p3


def _dot3(a, b):
    ah, al = _split2(a)
    bh, bl = _split2(b)
    return _dot(ah, bh) + _dot(ah, bl) + _dot(al, bh)


def _sel_dot(sel, x):
    p1, p2, p3 = _split3(x)
    return _dot(sel, p1) + _dot(sel, p2) + _dot(sel, p3)


def _dot_sel(x, sel):
    p1, p2, p3 = _split3(x)
    return _dot(p1, sel) + _dot(p2, sel) + _dot(p3, sel)


def _sigmoid(x):
    return 1.0 / (1.0 + jnp.exp(-x))


def _softplus(x):
    return jnp.maximum(x, 0.0) + jnp.log(1.0 + jnp.exp(-jnp.abs(x)))


def _silu(x):
    return x * _sigmoid(x)


def _gelu_tanh(x):
    return 0.5 * x * (1.0 + jnp.tanh(0.7978845608028654 * (x + 0.044715 * (x * x * x))))


def _rmsnorm(x, g):
    ms = jnp.mean(x * x, axis=-1, keepdims=True)
    return x * lax.rsqrt(ms + EPS) * g


def _seg_sum(x, e_ref, et_ref):
    hi, lo = _split2(x)
    hs = _dot(hi, e_ref[...]) + _dot(lo, e_ref[...])
    hh, hl = _split2(hs)
    return _dot(hh, et_ref[...]) + _dot(hl, et_ref[...])


def _params(sem):
    return pltpu.CompilerParams(dimension_semantics=sem, vmem_limit_bytes=VMEM_LIMIT)


def _mod_kernel(c_ref, w_ref, b_ref, o_ref):
    s = _silu(c_ref[...])
    o_ref[...] = _dot3(s, w_ref[...]) + b_ref[...]


def _modulation(cond, mod_w, mod_b):
    nl, d, n = mod_w.shape
    rows = cond.shape[0]
    tn = d
    return pl.pallas_call(
        _mod_kernel,
        grid=(nl, n // tn),
        in_specs=[
            pl.BlockSpec((rows, d), lambda l, j: (0, 0)),
            pl.BlockSpec((None, d, tn), lambda l, j: (l, 0, j)),
            pl.BlockSpec((None, 1, tn), lambda l, j: (l, 0, j)),
        ],
        out_specs=pl.BlockSpec((None, rows, tn), lambda l, j: (l, 0, j)),
        out_shape=jax.ShapeDtypeStruct((nl, rows, n), F32),
        compiler_params=_params(("arbitrary", "arbitrary")),
        name="modulation",
    )(cond, mod_w, mod_b.reshape(nl, 1, n))


def _inproj_kernel(x_ref, mod_ref, g_ref, w_ref, pa_ref, pb_ref, pg_ref, *, na, nb):
    h = _rmsnorm(x_ref[...], g_ref[...])
    h = h * (1.0 + mod_ref[0, 1:2, :]) + mod_ref[0, 0:1, :]
    hb = h.astype(BF16)
    pa_ref[...] = _dot(hb, w_ref[:, :na])
    pb_ref[...] = _dot(hb, w_ref[:, na:na + nb])
    pg_ref[...] = _dot(hb, w_ref[:, na + nb:])


def _inproj(x, mod_t, g, w, na, nb, tm):
    t, d = x.shape
    n = w.shape[1]
    ng = n - na - nb
    return pl.pallas_call(
        functools.partial(_inproj_kernel, na=na, nb=nb),
        grid=(t // tm,),
        in_specs=[
            pl.BlockSpec((tm, d), lambda i: (i, 0)),
            pl.BlockSpec((1, N_MOD, d), lambda i: (i, 0, 0)),
            pl.BlockSpec((1, d), lambda i: (0, 0)),
            pl.BlockSpec((d, n), lambda i: (0, 0)),
        ],
        out_specs=[
            pl.BlockSpec((tm, na), lambda i: (i, 0)),
            pl.BlockSpec((tm, nb), lambda i: (i, 0)),
            pl.BlockSpec((tm, ng), lambda i: (i, 0)),
        ],
        out_shape=[
            jax.ShapeDtypeStruct((t, na), F32),
            jax.ShapeDtypeStruct((t, nb), F32),
            jax.ShapeDtypeStruct((t, ng), F32),
        ],
        compiler_params=_params(("arbitrary",)),
        name="inproj",
    )(x, mod_t, g, w)


T_FWD, T_BWD, T_FIRST, T_LAST, T_SEQ = 0, 1, 2, 3, 4
T_FWD_PREV, T_FWD_NEXT, T_BWD_PREV, T_BWD_NEXT = 5, 6, 7, 8


def _scan_table(seq_lens, row_lens):
    rows = [[] for _ in range(9)]
    base = 0
    for s, (ls, rl) in enumerate(zip(seq_lens, row_lens)):
        assert ls % CHUNK == 0 and rl % CHUNK == 0 and ls % rl == 0
        nc = ls // CHUNK
        for n in range(nc):
            cf, cb = n, nc - 1 - n
            rows[T_FWD].append(base + cf)
            rows[T_BWD].append(base + cb)
            rows[T_FIRST].append(int(n == 0))
            rows[T_LAST].append(int(n == nc - 1))
            rows[T_SEQ].append(s)
            rows[T_FWD_PREV].append(int((cf * CHUNK) % rl != 0))
            rows[T_FWD_NEXT].append(int(((cf + 1) * CHUNK) % rl != 0))
            rows[T_BWD_PREV].append(int((cb * CHUNK) % rl != 0))
            rows[T_BWD_NEXT].append(int(((cb + 1) * CHUNK) % rl != 0))
        base += nc
    return jnp.asarray(np.array(rows, dtype=np.int32))


def _chunk_map(which):
    return lambda g, tab: (tab[which, g], 0)


def _halo_prev_map(which):
    return lambda g, tab: (jnp.maximum(tab[which, g] * (CHUNK // SUBLANES) - 1, 0), 0)


def _halo_next_map(which, n_blocks8):
    return lambda g, tab: (
        jnp.minimum((tab[which, g] + 1) * (CHUNK // SUBLANES), n_blocks8 - 1), 0)


def _rwkv_prologue(pa_ref, halo_ref, halo_ok, d, mu_ref, w0_ref, wup_ref, a0_ref, aup_ref,
                   kk_ref, ka_ref, rk_ref, e_ref, et_ref, bonus_ref, *, width, n_shift):
    reverse = bool(d)
    row = lax.broadcasted_iota(jnp.int32, (CHUNK, 1), 0)
    if reverse:
        edge, halo_row, shift = row == CHUNK - 1, 0, CHUNK - 1
    else:
        edge, halo_row, shift = row == 0, SUBLANES - 1, 1

    def mix(lo, hi):
        ps = pa_ref[:, lo:hi]
        nb = halo_ref[halo_row:halo_row + 1, lo:hi] * halo_ok
        shifted = jnp.where(edge, nb, pltpu.roll(ps, shift, 0))
        return ps + (shifted - ps) * mu_ref[d, :, lo:hi]

    r = mix(0, width)
    k = mix(width, 2 * width)
    v = mix(2 * width, 3 * width)
    wa = mix(3 * width, n_shift)

    lw = w0_ref[d] + _dot(jnp.tanh(wa).astype(BF16), wup_ref[d])
    logw = -jnp.exp(-_softplus(-lw) - 0.5)
    a = _sigmoid(a0_ref[d] + _dot(wa.astype(BF16), aup_ref[d]))

    kk = k * kk_ref[...]
    nrm = jnp.sqrt(_seg_sum(kk * kk, e_ref, et_ref))
    kk = kk / jnp.maximum(nrm, 1e-12)
    k = k * (1.0 + (a - 1.0) * ka_ref[...])
    bonus_ref[...] = _seg_sum(r * k * rk_ref[...], e_ref, et_ref) * v

    ti = lax.broadcasted_iota(jnp.int32, (CHUNK, CHUNK), 0)
    si = lax.broadcasted_iota(jnp.int32, (CHUNK, CHUNK), 1)
    tri = jnp.where((si >= ti) if reverse else (si <= ti), 1.0, 0.0).astype(BF16)
    gc = _sel_dot(tri, logw)
    g_end = gc[0:1, :] if reverse else gc[CHUNK - 1:CHUNK, :]

    beta = kk * a
    inv = jnp.exp(-gc)
    to_end = jnp.exp(g_end - gc)
    return dict(
        v=v, r_t=r * jnp.exp(gc), a_t=-kk * jnp.exp(gc - logw), b_t=beta * inv, k_t=k * inv,
        b_h=beta * to_end, k_h=k * to_end, dec_end=jnp.exp(g_end))


def _rwkv_kernel(tab_ref, paf_ref, halof_ref, pab_ref, halob_ref, s0_ref, mu_ref, w0_ref,
                 wup_ref, a0_ref, aup_ref, kk_ref, ka_ref, rk_ref, e_ref, et_ref, rep_ref,
                 rept_ref, yf_ref, bonf_ref, yb_ref, bonb_ref, sfin_ref, s_scr, *, width,
                 n_shift):
    g = pl.program_id(0)
    n_groups = width // GROUP
    cat = jnp.concatenate
    same_head = (lax.broadcasted_iota(jnp.int32, (GROUP, GROUP), 0) // HEAD_DIM
                 == lax.broadcasted_iota(jnp.int32, (GROUP, GROUP), 1) // HEAD_DIM)

    @pl.when(tab_ref[T_FIRST, g] == 1)
    def _():
        for d in range(N_DIR):
            for q in range(n_groups):
                tiled = _dot_sel(s0_ref[0, d, q * GROUP:(q + 1) * GROUP, :], rep_ref[...])
                s_scr[d, q] = jnp.where(same_head, tiled, 0.0)

    shared = (mu_ref, w0_ref, wup_ref, a0_ref, aup_ref, kk_ref, ka_ref, rk_ref, e_ref, et_ref)
    pro = [
        _rwkv_prologue(paf_ref, halof_ref, tab_ref[T_FWD_PREV, g].astype(F32), 0, *shared,
                       bonf_ref, width=width, n_shift=n_shift),
        _rwkv_prologue(pab_ref, halob_ref, tab_ref[T_BWD_NEXT, g].astype(F32), 1, *shared,
                       bonb_ref, width=width, n_shift=n_shift),
    ]
    y_refs = (yf_ref, yb_ref)

    lane_head = lax.broadcasted_iota(jnp.int32, (1, GROUP), 1) // HEAD_DIM
    head_masks = [lane_head == h for h in range(GROUP // HEAD_DIM)]

    def bd(x):
        return cat([jnp.where(m, x, 0.0) for m in head_masks], axis=0).astype(BF16)

    t_row = lax.broadcasted_iota(jnp.int32, (CHUNK, GROUP), 0)
    s_lane = lax.broadcasted_iota(jnp.int32, (CHUNK, GROUP), 1) % CHUNK
    strict = (s_lane < t_row, s_lane > t_row)
    incl = (s_lane <= t_row, s_lane >= t_row)
    eye = jnp.where(s_lane == t_row, 1.0, 0.0)

    items = [(d, q, slice(q * GROUP, (q + 1) * GROUP))
             for d in range(N_DIR) for q in range(n_groups)]
    n = range(len(items))

    lhs_ar = [cat([pro[d]["a_t"][:, s], pro[d]["r_t"][:, s]], axis=0).astype(BF16)
              for d, q, s in items]
    v_bd = [bd(pro[d]["v"][:, s]) for d, q, s in items]
    aa = [_dot_nt(lhs_ar[i], cat([bd(pro[d]["b_t"][:, s]), bd(pro[d]["k_t"][:, s])], axis=0))
          for i, (d, q, s) in enumerate(items)]
    a_ab = [jnp.where(strict[d], aa[i][:CHUNK, :GROUP], 0.0) for i, (d, q, s) in enumerate(items)]
    a_ak = [jnp.where(strict[d], aa[i][:CHUNK, GROUP:], 0.0).astype(BF16)
            for i, (d, q, s) in enumerate(items)]
    a_r = [cat([jnp.where(incl[d], aa[i][CHUNK:, :GROUP], 0.0),
                jnp.where(incl[d], aa[i][CHUNK:, GROUP:], 0.0)], axis=1).astype(BF16)
           for i, (d, q, s) in enumerate(items)]
    s_old = [s_scr[d, q] for d, q, s in items]
    s_lhs = [_dot_nt(lhs_ar[i], s_old[i].astype(BF16)) for i in n]
    rhs = [s_lhs[i][:CHUNK] + _dot(a_ak[i], v_bd[i]) for i in n]

    inv_m = [eye + x for x in a_ab]
    pw = [_dot(x.astype(BF16), bd(x)) for x in a_ab]
    for _ in range(CHUNK.bit_length() - 3):
        both = [_dot(cat([pw[i], inv_m[i]], axis=0).astype(BF16), bd(pw[i])) for i in n]
        pw = [x[:CHUNK] for x in both]
        inv_m = [inv_m[i] + both[i][CHUNK:] for i in n]
    inv_m = [inv_m[i] + _dot(inv_m[i].astype(BF16), bd(pw[i])) for i in n]

    u = [_dot(inv_m[i].astype(BF16), bd(rhs[i])) for i in n]
    for i, (d, q, s) in enumerate(items):
        y_refs[d][:, s] = s_lhs[i][CHUNK:] + _dot(a_r[i], cat([bd(u[i]), v_bd[i]], axis=0))
    for i, (d, q, s) in enumerate(items):
        upd = _dot_tn(cat([u[i], pro[d]["v"][:, s]], axis=0).astype(BF16),
                      cat([pro[d]["b_h"][:, s], pro[d]["k_h"][:, s]], axis=0).astype(BF16))
        s_scr[d, q] = s_old[i] * pro[d]["dec_end"][:, s] + jnp.where(same_head, upd, 0.0)

    @pl.when(tab_ref[T_LAST, g] == 1)
    def _():
        for d in range(N_DIR):
            for q in range(n_groups):
                sfin_ref[0, d, q * GROUP:(q + 1) * GROUP, :] = _dot_sel(s_scr[d, q],
                                                                          rept_ref[...])


def _rwkv_scan(tab, pa, s0, mu, w0, wup, a0, aup, k_k, k_a, r_k, e, et, *, width, n_shift):
    t, na = pa.shape
    n_steps = tab.shape[1]
    n_seq = s0.shape[0]
    n_groups = width // GROUP
    rep = np.tile(np.eye(HEAD_DIM, dtype=np.float32), (1, GROUP // HEAD_DIM))
    c2 = lambda g, tab: (0, 0)
    c3 = lambda g, tab: (0, 0, 0)
    vec = pl.BlockSpec((1, width), c2)
    dvec = pl.BlockSpec((N_DIR, 1, width), c3)
    dmat = pl.BlockSpec((N_DIR, LANES, width), c3)
    state_spec = pl.BlockSpec((1, N_DIR, width, HEAD_DIM), lambda g, tab: (tab[T_SEQ, g], 0, 0, 0))
    fwd_out = pl.BlockSpec((CHUNK, width), _chunk_map(T_FWD))
    bwd_out = pl.BlockSpec((CHUNK, width), _chunk_map(T_BWD))
    grid_spec = pltpu.PrefetchScalarGridSpec(
        num_scalar_prefetch=1,
        grid=(n_steps,),
        in_specs=[
            pl.BlockSpec((CHUNK, na), _chunk_map(T_FWD)),
            pl.BlockSpec((SUBLANES, na), _halo_prev_map(T_FWD)),
            pl.BlockSpec((CHUNK, na), _chunk_map(T_BWD)),
            pl.BlockSpec((SUBLANES, na), _halo_next_map(T_BWD, t // SUBLANES)),
            state_spec,
            pl.BlockSpec((N_DIR, 1, n_shift), c3),
            dvec, dmat, dvec, dmat,
            vec, vec, vec,
            pl.BlockSpec((width, LANES), c2),
            pl.BlockSpec((LANES, width), c2),
            pl.BlockSpec((HEAD_DIM, GROUP), c2),
            pl.BlockSpec((GROUP, HEAD_DIM), c2),
        ],
        out_specs=[fwd_out, fwd_out, bwd_out, bwd_out, state_spec],
        scratch_shapes=[pltpu.VMEM((N_DIR, n_groups, GROUP, GROUP), F32)],
    )
    tok = jax.ShapeDtypeStruct((t, width), F32)
    return pl.pallas_call(
        functools.partial(_rwkv_kernel, width=width, n_shift=n_shift),
        grid_spec=grid_spec,
        out_shape=[tok, tok, tok, tok,
                   jax.ShapeDtypeStruct((n_seq, N_DIR, width, HEAD_DIM), F32)],
        compiler_params=_params(("arbitrary",)),
        name="rwkv_scan",
    )(tab, pa, pa, pa, pa, s0, mu, w0, wup, a0, aup, k_k, k_a, r_k, e, et,
      jnp.asarray(rep, BF16), jnp.asarray(rep.T.copy(), BF16))


def _lru_gates(pb_ref, hp_ref, hn_ref, prev_ok, next_ok, d, cw_ref, cb_ref, wg_ref, brg_ref,
               big_ref, sp, a_scr, u_scr, *, width):
    row = lax.broadcasted_iota(jnp.int32, (CHUNK, 1), 0)
    for n in range(width // LANES):
        sl = slice(n * LANES, (n + 1) * LANES)
        x = pb_ref[:, sl]
        p7 = hp_ref[SUBLANES - 1:SUBLANES, sl] * prev_ok
        p6 = hp_ref[SUBLANES - 2:SUBLANES - 1, sl] * prev_ok
        n0 = hn_ref[0:1, sl] * next_ok
        xm1 = jnp.where(row == 0, p7, pltpu.roll(x, 1, 0))
        xm2 = jnp.where(row == 0, p6, jnp.where(row == 1, p7, pltpu.roll(x, 2, 0)))
        xp1 = jnp.where(row == CHUNK - 1, n0, pltpu.roll(x, CHUNK - 1, 0))
        xc = (xm2 * cw_ref[0:1, sl] + xm1 * cw_ref[1:2, sl] + x * cw_ref[2:3, sl]
              + xp1 * cw_ref[3:4, sl] + cb_ref[:, sl])
        gz = _dot(xc.astype(BF16), wg_ref[d, n])
        rg = _sigmoid(gz[:, :LANES] + brg_ref[d, :, sl])
        ig = _sigmoid(gz[:, LANES:] + big_ref[d, :, sl])
        log_a = (-LRU_C) * rg * sp[:, sl]
        a_scr[d, :, sl] = jnp.exp(log_a)
        u_scr[d, :, sl] = jnp.sqrt(1.0 - jnp.exp(2.0 * log_a)) * ig * xc


def _lru_kernel(tab_ref, pbf_ref, hpf_ref, hnf_ref, pbb_ref, hpb_ref, hnb_ref, h0_ref, cw_ref,
                cb_ref, wg_ref, brg_ref, big_ref, lam_ref, hf_ref, hb_ref, hfin_ref,
                a_scr, u_scr, h_scr, *, width):
    g = pl.program_id(0)

    @pl.when(tab_ref[T_FIRST, g] == 1)
    def _():
        h_scr[...] = h0_ref[0]

    flag = lambda which: tab_ref[which, g].astype(F32)
    common = dict(width=width)
    _lru_gates(pbf_ref, hpf_ref, hnf_ref, flag(T_FWD_PREV), flag(T_FWD_NEXT), 0, cw_ref, cb_ref,
               wg_ref, brg_ref, big_ref, _softplus(-lam_ref[0]), a_scr, u_scr, **common)
    _lru_gates(pbb_ref, hpb_ref, hnb_ref, flag(T_BWD_PREV), flag(T_BWD_NEXT), 1, cw_ref, cb_ref,
               wg_ref, brg_ref, big_ref, _softplus(-lam_ref[1]), a_scr, u_scr, **common)

    hf = h_scr[0:1, :]
    hb = h_scr[1:2, :]
    for i in range(CHUNK):
        tf, tb = i, CHUNK - 1 - i
        hf = a_scr[0, tf:tf + 1, :] * hf + u_scr[0, tf:tf + 1, :]
        hf_ref[tf:tf + 1, :] = hf
        hb = a_scr[1, tb:tb + 1, :] * hb + u_scr[1, tb:tb + 1, :]
        hb_ref[tb:tb + 1, :] = hb
    h_scr[0:1, :] = hf
    h_scr[1:2, :] = hb

    @pl.when(tab_ref[T_LAST, g] == 1)
    def _():
        hfin_ref[0] = h_scr[...]


def _lru_scan(tab, pb, h0, cw, cb, wg, brg, big, lam, *, width):
    t, nb = pb.shape
    n_steps = tab.shape[1]
    n_seq = h0.shape[0]
    nblk = width // LANES
    n8 = t // SUBLANES
    c2 = lambda g, tab: (0, 0)
    c3 = lambda g, tab: (0, 0, 0)
    dvec = pl.BlockSpec((N_DIR, 1, width), c3)
    state_spec = pl.BlockSpec((1, N_DIR, width), lambda g, tab: (tab[T_SEQ, g], 0, 0))
    grid_spec = pltpu.PrefetchScalarGridSpec(
        num_scalar_prefetch=1,
        grid=(n_steps,),
        in_specs=[
            pl.BlockSpec((CHUNK, nb), _chunk_map(T_FWD)),
            pl.BlockSpec((SUBLANES, nb), _halo_prev_map(T_FWD)),
            pl.BlockSpec((SUBLANES, nb), _halo_next_map(T_FWD, n8)),
            pl.BlockSpec((CHUNK, nb), _chunk_map(T_BWD)),
            pl.BlockSpec((SUBLANES, nb), _halo_prev_map(T_BWD)),
            pl.BlockSpec((SUBLANES, nb), _halo_next_map(T_BWD, n8)),
            state_spec,
            pl.BlockSpec(cw.shape, c2),
            pl.BlockSpec((1, width), c2),
            pl.BlockSpec((N_DIR, nblk, LANES, 2 * LANES), lambda g, tab: (0, 0, 0, 0)),
            dvec, dvec, dvec,
        ],
        out_specs=[
            pl.BlockSpec((CHUNK, width), _chunk_map(T_FWD)),
            pl.BlockSpec((CHUNK, width), _chunk_map(T_BWD)),
            state_spec,
        ],
        scratch_shapes=[pltpu.VMEM((N_DIR, CHUNK, width), F32),
                        pltpu.VMEM((N_DIR, CHUNK, width), F32),
                        pltpu.VMEM((N_DIR, width), F32)],
    )
    return pl.pallas_call(
        functools.partial(_lru_kernel, width=width),
        grid_spec=grid_spec,
        out_shape=[
            jax.ShapeDtypeStruct((t, width), F32),
            jax.ShapeDtypeStruct((t, width), F32),
            jax.ShapeDtypeStruct((n_seq, N_DIR, width), F32),
        ],
        compiler_params=_params(("arbitrary",)),
        name="lru_scan",
    )(tab, pb, pb, pb, pb, pb, pb, h0, cw, cb, wg, brg, big, lam)


def _postmix_kernel(yf_ref, yb_ref, bf_ref, bb_ref, gd_ref, hf_ref, hb_ref, lg_ref, pg_ref,
                    x_ref, mod_ref, lnw_ref, lnb_ref, gup_ref, wa_ref, wb_ref, wo_ref, bm_ref,
                    gpost_ref, gpre_ref, e_ref, et_ref, *rest, d_model, n_experts):
    if n_experts:
        router_ref, x1_ref, h2_ref, gates_ref = rest
    else:
        x1_ref, h2_ref = rest
    y = yf_ref[...] + yb_ref[...]
    inv_d = 1.0 / HEAD_DIM
    mean = _seg_sum(y, e_ref, et_ref) * inv_d
    yc = y - mean
    var = _seg_sum(yc * yc, e_ref, et_ref) * inv_d
    yn = yc * lax.rsqrt(var + LNX_EPS) * lnw_ref[...] + lnb_ref[...]
    gate = _dot(_sigmoid(gd_ref[...]).astype(BF16), gup_ref[...])
    ya = (yn + bf_ref[...] + bb_ref[...]) * gate
    yl = (hf_ref[...] + hb_ref[...]) * _gelu_tanh(lg_ref[...])
    proj_a = _dot(ya.astype(BF16), wa_ref[...])
    proj_b = _dot(yl.astype(BF16), wb_ref[...])
    mg = _sigmoid(pg_ref[...] + bm_ref[...])
    m = mg[:, :d_model] * proj_a + mg[:, d_model:] * proj_b
    o = _dot(m.astype(BF16), wo_ref[...])
    x1 = x_ref[...] + mod_ref[0, 2:3, :] * _rmsnorm(o, gpost_ref[...])
    x1_ref[...] = x1
    h2 = _rmsnorm(x1, gpre_ref[...]) * (1.0 + mod_ref[0, 4:5, :]) + mod_ref[0, 3:4, :]
    h2_ref[...] = h2.astype(BF16)
    if n_experts:
        lane = lax.broadcasted_iota(jnp.int32, (1, LANES), 1).astype(F32)
        logits = jnp.where(lane < n_experts, _dot3(h2, router_ref[...]), -jnp.inf)
        m1 = jnp.max(logits, axis=-1, keepdims=True)
        i1 = jnp.min(jnp.where(logits == m1, lane, float(LANES)), axis=-1, keepdims=True)
        rest_l = jnp.where(lane == i1, -jnp.inf, logits)
        m2 = jnp.max(rest_l, axis=-1, keepdims=True)
        i2 = jnp.min(jnp.where(rest_l == m2, lane, float(LANES)), axis=-1, keepdims=True)
        e2 = jnp.exp(m2 - m1)
        w1 = 1.0 / (1.0 + e2)
        gates = jnp.where(lane == i1, w1, 0.0) + jnp.where(lane == i2, e2 * w1, 0.0)
        gates_ref[...] = gates.T[:SUBLANES]


def _postmix(yf, yb, bf, bb, pa, hf, hb, pb, pg, x, mod_t, lnw, lnb, gup, wa, wb, wo, bm, gpost,
             gpre, e, et, router, *, tm, gd_block):
    t, d = x.shape
    width = yf.shape[1]
    lru_w = hf.shape[1]
    n_experts = 0 if router is None else router.shape[1]
    row = lambda i: (i, 0)
    const = lambda i: (0, 0)
    vec_d = pl.BlockSpec((1, d), const)
    vec_w = pl.BlockSpec((1, width), const)
    tile_w = pl.BlockSpec((tm, width), row)
    tile_l = pl.BlockSpec((tm, lru_w), row)
    in_specs = [
        tile_w, tile_w, tile_w, tile_w,
        pl.BlockSpec((tm, LANES), lambda i: (i, gd_block)),
        tile_l, tile_l,
        pl.BlockSpec((tm, lru_w), lambda i: (i, 1)),
        pl.BlockSpec((tm, 2 * d), row),
        pl.BlockSpec((tm, d), row),
        pl.BlockSpec((1, N_MOD, d), lambda i: (i, 0, 0)),
        vec_w, vec_w,
        pl.BlockSpec((LANES, width), const),
        pl.BlockSpec((width, d), const),
        pl.BlockSpec((lru_w, d), const),
        pl.BlockSpec((d, d), const),
        pl.BlockSpec((1, 2 * d), const),
        vec_d, vec_d,
        pl.BlockSpec((width, LANES), const),
        pl.BlockSpec((LANES, width), const),
    ]
    args = [yf, yb, bf, bb, pa, hf, hb, pb, pg, x, mod_t, lnw, lnb, gup, wa, wb, wo, bm, gpost,
            gpre, e, et]
    out_specs = [pl.BlockSpec((tm, d), row), pl.BlockSpec((tm, d), row)]
    out_shape = [jax.ShapeDtypeStruct((t, d), F32), jax.ShapeDtypeStruct((t, d), BF16)]
    if n_experts:
        assert n_experts <= SUBLANES
        router_pad = jnp.zeros((d, LANES), F32).at[:, :n_experts].set(router)
        in_specs.append(pl.BlockSpec((d, LANES), const))
        args.append(router_pad)
        out_specs.append(pl.BlockSpec((SUBLANES, tm), lambda i: (0, i)))
        out_shape.append(jax.ShapeDtypeStruct((SUBLANES, t), F32))
    return pl.pallas_call(
        functools.partial(_postmix_kernel, d_model=d, n_experts=n_experts),
        grid=(t // tm,),
        in_specs=in_specs,
        out_specs=out_specs,
        out_shape=out_shape,
        compiler_params=_params(("arbitrary",)),
        name="postmix",
    )(*args)


def _ffn_kernel(h_ref, x1_ref, mod_ref, gpost_ref, w1_ref, w3_ref, w2_ref, o_ref, acc_ref):
    f = pl.program_id(1)

    @pl.when(f == 0)
    def _():
        acc_ref[...] = jnp.zeros_like(acc_ref)

    h = h_ref[...]
    hid = _silu(_dot(h, w1_ref[...])) * _dot(h, w3_ref[...])
    acc_ref[...] += _dot(hid.astype(BF16), w2_ref[...])

    @pl.when(f == pl.num_programs(1) - 1)
    def _():
        o_ref[...] = x1_ref[...] + mod_ref[0, 5:6, :] * _rmsnorm(acc_ref[...], gpost_ref[...])


def _ffn(h2, x1, mod_t, gpost, w1, w3, w2, *, tm, tf):
    t, d = x1.shape
    ff = w1.shape[1]
    row = lambda i, f: (i, 0)
    return pl.pallas_call(
        _ffn_kernel,
        grid=(t // tm, ff // tf),
        in_specs=[
            pl.BlockSpec((tm, d), row),
            pl.BlockSpec((tm, d), row),
            pl.BlockSpec((1, N_MOD, d), lambda i, f: (i, 0, 0)),
            pl.BlockSpec((1, d), lambda i, f: (0, 0)),
            pl.BlockSpec((d, tf), lambda i, f: (0, f)),
            pl.BlockSpec((d, tf), lambda i, f: (0, f)),
            pl.BlockSpec((tf, d), lambda i, f: (f, 0)),
        ],
        out_specs=pl.BlockSpec((tm, d), row),
        out_shape=jax.ShapeDtypeStruct((t, d), F32),
        scratch_shapes=[pltpu.VMEM((tm, d), F32)],
        compiler_params=_params(("arbitrary", "arbitrary")),
        name="ffn_dense",
    )(h2, x1, mod_t, gpost, w1, w3, w2)


def _moe_kernel(gt_ref, before_ref, h_ref, x1_ref, mod_ref, gpost_ref, w1_ref, w3_ref, w2_ref,
                o_ref, pos_scr, x_scr, y_scr):
    e = pl.program_id(1)
    f = pl.program_id(2)
    last_f = pl.num_programs(2) - 1

    @pl.when((e == 0) & (f == 0))
    def _():
        chosen_all = jnp.where(gt_ref[...] > 0.0, 1.0, 0.0).astype(BF16)
        pos_scr[...] = _dot(chosen_all, before_ref[...])
        o_ref[...] = jnp.zeros_like(o_ref)

    is_e = lax.broadcasted_iota(jnp.int32, (SUBLANES, 1), 0) == e
    gate_row = jnp.sum(jnp.where(is_e, gt_ref[...], 0.0), axis=0, keepdims=True)
    pos_row = jnp.sum(jnp.where(is_e, pos_scr[...], 0.0), axis=0, keepdims=True)
    chosen = gate_row > 0.0
    slot_row = jnp.where(chosen, pos_row, -1.0)
    count = jnp.sum(jnp.where(chosen, 1.0, 0.0)).astype(jnp.int32)
    n_sub = (count + (MOE_SUB - 1)) // MOE_SUB

    def body(j, carry):
        r0 = pl.multiple_of(j * MOE_SUB, MOE_SUB)
        slots = (r0 + lax.broadcasted_iota(jnp.int32, (MOE_SUB, 1), 0)).astype(F32)
        onehot = slot_row == slots
        p = jnp.where(onehot, 1.0, 0.0).astype(BF16)

        @pl.when(f == 0)
        def _():
            x_scr[pl.ds(r0, MOE_SUB), :] = _dot(p, h_ref[...]).astype(BF16)

        x = x_scr[pl.ds(r0, MOE_SUB), :]
        hid = _silu(_dot(x, w1_ref[...])) * _dot(x, w3_ref[...])
        g_col = jnp.sum(jnp.where(onehot, gate_row, 0.0), axis=1, keepdims=True)
        part = _dot((hid * g_col).astype(BF16), w2_ref[...])

        @pl.when(f == 0)
        def _():
            y_scr[pl.ds(r0, MOE_SUB), :] = part

        @pl.when(f > 0)
        def _():
            y_scr[pl.ds(r0, MOE_SUB), :] += part

        @pl.when(f == last_f)
        def _():
            hi, lo = _split2(y_scr[pl.ds(r0, MOE_SUB), :])
            o_ref[...] += _dot_tn(jnp.concatenate([p, p], axis=0),
                                  jnp.concatenate([hi, lo], axis=0))
        return carry

    lax.fori_loop(0, n_sub, body, 0)

    @pl.when((e == pl.num_programs(1) - 1) & (f == last_f))
    def _():
        o_ref[...] = x1_ref[...] + mod_ref[0, 5:6, :] * _rmsnorm(o_ref[...], gpost_ref[...])


def _moe(gates_t, h2, x1, mod_t, gpost, w1, w3, w2, *, tb, tf):
    t, d = x1.shape
    n_e, _, ff = w1.shape
    before = jnp.asarray(np.triu(np.ones((tb, tb), np.float32), 1), BF16)
    row = lambda b, e, f: (b, 0)
    return pl.pallas_call(
        _moe_kernel,
        grid=(t // tb, n_e, ff // tf),
        in_specs=[
            pl.BlockSpec((SUBLANES, tb), lambda b, e, f: (0, b)),
            pl.BlockSpec((tb, tb), lambda b, e, f: (0, 0)),
            pl.BlockSpec((tb, d), row),
            pl.BlockSpec((tb, d), row),
            pl.BlockSpec((1, N_MOD, d), lambda b, e, f: (b, 0, 0)),
            pl.BlockSpec((1, d), lambda b, e, f: (0, 0)),
            pl.BlockSpec((None, d, tf), lambda b, e, f: (e, 0, f)),
            pl.BlockSpec((None, d, tf), lambda b, e, f: (e, 0, f)),
            pl.BlockSpec((None, tf, d), lambda b, e, f: (e, f, 0)),
        ],
        out_specs=pl.BlockSpec((tb, d), row),
        out_shape=jax.ShapeDtypeStruct((t, d), F32),
        scratch_shapes=[pltpu.VMEM((SUBLANES, tb), F32), pltpu.VMEM((tb, d), BF16),
                        pltpu.VMEM((tb, d), F32)],
        compiler_params=_params(("arbitrary", "arbitrary", "arbitrary")),
        name="ffn_moe",
    )(gates_t, before, h2, x1, mod_t, gpost, w1, w3, w2)


def _head_indicator(width):
    e = np.zeros((width, LANES), np.float32)
    e[np.arange(width), np.arange(width) // HEAD_DIM] = 1.0
    return jnp.asarray(e, BF16), jnp.asarray(e.T.copy(), BF16)


def _tile_mod(mod_l, tile_rows, tm, d):
    idx = np.asarray(tile_rows[::tm], np.int32)
    return jnp.take(mod_l, idx, axis=0).reshape(len(idx), N_MOD, d)


def _pick_tile(t, pref):
    while t % pref:
        pref //= 2
    return pref


def kernel(x_prompt, x_sample, c, c_ctx, state_rwkv, state_lru, mod_w, mod_b, norm_pre_mix, norm_post_mix, norm_pre_ffn, norm_post_ffn, w_in, b_merge, rwkv_mu, rwkv_w0, rwkv_w_up, rwkv_a0, rwkv_a_up, rwkv_k_k, rwkv_k_a, rwkv_r_k, rwkv_g_up, rwkv_lnx_w, rwkv_lnx_b, lru_conv_w, lru_conv_b, lru_w_rg, lru_b_rg, lru_w_ig, lru_b_ig, lru_lam, w_proj_a, w_proj_b, w_out, ffn_w1, ffn_w3, ffn_w2, moe_router, moe_w1, moe_w3, moe_w2):
    b_p, seq, d = x_prompt.shape
    b_s, dec_seq, _ = x_sample.shape
    depth = w_in.shape[0]
    n_heads, head_dim = rwkv_r_k.shape[1], rwkv_r_k.shape[2]
    width = n_heads * head_dim
    n_shift = rwkv_mu.shape[2]
    lru_w = lru_lam.shape[2]
    blk = lru_w_rg.shape[3]
    gate_lora = rwkv_g_up.shape[1]
    lora = rwkv_w_up.shape[2]
    na = n_shift + gate_lora
    nb = 2 * lru_w
    assert head_dim == HEAD_DIM and blk == LANES and rwkv_mu.shape[1] == N_DIR
    assert n_shift == 3 * width + 2 * lora and 2 * lora == LANES and gate_lora == LANES
    assert dec_seq % GRID_W == 0 and lru_conv_w.shape[1] == 4 and width % GROUP == 0

    t_p, t_s = b_p * seq, b_s * dec_seq
    x = jnp.concatenate([x_prompt.reshape(t_p, d), x_sample.reshape(t_s, d)], axis=0)

    n_cond = -(-(1 + b_s) // 16) * 16
    cond = jnp.zeros((n_cond, d), F32).at[0].set(c_ctx).at[1:1 + b_s].set(c)
    mod = _modulation(cond, mod_w, mod_b)
    tile_rows = np.concatenate([np.zeros(t_p, np.int32),
                                1 + np.arange(t_s, dtype=np.int32) // dec_seq])

    tab = _scan_table([seq] * b_p + [dec_seq] * b_s, [seq] * b_p + [GRID_W] * b_s)
    e_ind, et_ind = _head_indicator(width)

    tm = _pick_tile(np.gcd(seq, dec_seq), 256)
    tm_ffn = _pick_tile(np.gcd(t_p, dec_seq), 512)
    tb_moe = _pick_tile(np.gcd(t_p, dec_seq), 1024)

    rwkv_fin, lru_fin = [], []
    for l in range(depth):
        mod_t = _tile_mod(mod[l], tile_rows, tm, d)
        pa, pb, pg = _inproj(x, mod_t, norm_pre_mix[l][None], w_in[l].astype(BF16), na, nb, tm)

        s0 = jnp.concatenate(
            [jnp.zeros((b_p, N_DIR, width, HEAD_DIM), F32),
             state_rwkv[:, l].astype(F32).reshape(b_s, N_DIR, width, HEAD_DIM)], axis=0)
        zpad = jnp.zeros((N_DIR, lora, width), F32)
        wup = jnp.concatenate([rwkv_w_up[l], zpad], axis=1).astype(BF16)
        aup = jnp.concatenate([zpad, rwkv_a_up[l]], axis=1).astype(BF16)
        yf, bonf, yb, bonb, sfin = _rwkv_scan(
            tab, pa, s0, rwkv_mu[l][:, None], rwkv_w0[l][:, None], wup, rwkv_a0[l][:, None], aup,
            rwkv_k_k[l][None], rwkv_k_a[l][None], rwkv_r_k[l].reshape(1, width), e_ind, et_ind,
            width=width, n_shift=n_shift)
        rwkv_fin.append(sfin[:b_p].reshape(b_p, N_DIR, n_heads, HEAD_DIM, HEAD_DIM))

        h0 = jnp.concatenate([jnp.zeros((b_p, N_DIR, lru_w), F32),
                              state_lru[:, l].astype(F32)], axis=0)
        wg = jnp.concatenate([lru_w_rg[l], lru_w_ig[l]], axis=-1).astype(BF16)
        hf, hb, hfin = _lru_scan(
            tab, pb, h0, lru_conv_w[l], lru_conv_b[l][None], wg, lru_b_rg[l][:, None],
            lru_b_ig[l][:, None], lru_lam[l][:, None], width=lru_w)
        lru_fin.append(hfin[:b_p])

        is_moe = l % 2 == 1
        i = l // 2
        post = _postmix(
            yf, yb, bonf, bonb, pa, hf, hb, pb, pg, x, mod_t,
            rwkv_lnx_w[l][None], rwkv_lnx_b[l][None], rwkv_g_up[l].astype(BF16),
            w_proj_a[l].astype(BF16), w_proj_b[l].astype(BF16), w_out[l].astype(BF16),
            b_merge[l][None], norm_post_mix[l][None], norm_pre_ffn[l][None], e_ind, et_ind,
            moe_router[i] if is_moe else None, tm=tm, gd_block=n_shift // LANES)
        if is_moe:
            x1, h2, gates_t = post
            ff = moe_w1.shape[3]
            x = _moe(gates_t, h2, x1, _tile_mod(mod[l], tile_rows, tb_moe, d),
                     norm_post_ffn[l][None], moe_w1[i].astype(BF16), moe_w3[i].astype(BF16),
                     moe_w2[i].astype(BF16), tb=tb_moe,
                     tf=ff // 2 if ff % (2 * LANES) == 0 else ff)
        else:
            x1, h2 = post
            ff = ffn_w1.shape[2]
            x = _ffn(h2, x1, _tile_mod(mod[l], tile_rows, tm_ffn, d), norm_post_ffn[l][None],
                     ffn_w1[i].astype(BF16), ffn_w3[i].astype(BF16), ffn_w2[i].astype(BF16),
                     tm=tm_ffn, tf=_pick_tile(ff, 1024))

    y_prompt = x[:t_p].reshape(b_p, seq, d).astype(x_prompt.dtype)
    y_sample = x[t_p:].reshape(b_s, dec_seq, d).astype(x_sample.dtype)
    new_state_rwkv = jnp.stack(rwkv_fin, axis=1).astype(x_prompt.dtype)
    new_state_lru = jnp.stack(lru_fin, axis=1).astype(x_prompt.dtype)
    return (y_prompt, y_sample, new_state_rwkv, new_state_lru)
```

```python
import functools

import numpy as np
import jax
import jax.numpy as jnp
from jax import lax
from jax.experimental import pallas as pl
from jax.experimental.pallas import tpu as pltpu

F32 = jnp.float32
BF16 = jnp.bfloat16

GRID_W = 64
CHUNK = 64
LANES = 128
SUBLANES = 8
HEAD_DIM = 64
GROUP = 256
MOE_SUB = 128
LRU_C = 8.0
EPS = 1e-6
LNX_EPS = 64e-5
N_MOD = 6
N_DIR = 2
VMEM_LIMIT = 56 * 1024 * 1024


def _dot(a, b):
    return jnp.dot(a, b, preferred_element_type=F32)


def _dot_nt(a, b):
    return lax.dot_general(a, b, (((1,), (1,)), ((), ())), preferred_element_type=F32)


def _dot_tn(a, b):
    return lax.dot_general(a, b, (((0,), (0,)), ((), ())), preferred_element_type=F32)


def _split2(x):
    hi = x.astype(BF16)
    lo = (x - hi.astype(F32)).astype(BF16)
    return hi, lo


def _split3(x):
    p1 = x.astype(BF16)
    rem = x - p1.astype(F32)
    p2 = rem.astype(BF16)
    p3 = (rem - p2.astype(F32)).astype(BF16)
    return p1, p2, p3


def _dot3(a, b):
    ah, al = _split2(a)
    bh, bl = _split2(b)
    return _dot(ah, bh) + _dot(ah, bl) + _dot(al, bh)


def _sel_dot(sel, x):
    p1, p2, p3 = _split3(x)
    return _dot(sel, p1) + _dot(sel, p2) + _dot(sel, p3)


def _dot_sel(x, sel):
    p1, p2, p3 = _split3(x)
    return _dot(p1, sel) + _dot(p2, sel) + _dot(p3, sel)


def _sigmoid(x):
    return 0.5 * jnp.tanh(0.5 * x) + 0.5


def _softplus(x):
    return jnp.maximum(x, 0.0) + jnp.log(1.0 + jnp.exp(-jnp.abs(x)))


def _silu(x):
    return x * _sigmoid(x)


def _gelu_tanh(x):
    return 0.5 * x * (1.0 + jnp.tanh(0.7978845608028654 * (x + 0.044715 * (x * x * x))))


def _rmsnorm(x, g):
    ms = jnp.mean(x * x, axis=-1, keepdims=True)
    return x * lax.rsqrt(ms + EPS) * g


def _seg_sum(x, e_ref, et_ref):
    hi, lo = _split2(x)
    hs = _dot(hi, e_ref[...]) + _dot(lo, e_ref[...])
    hh, hl = _split2(hs)
    return _dot(hh, et_ref[...]) + _dot(hl, et_ref[...])


def _params(sem):
    return pltpu.CompilerParams(dimension_semantics=sem, vmem_limit_bytes=VMEM_LIMIT)


def _mod_kernel(c_ref, w_ref, b_ref, o_ref):
    s = _silu(c_ref[...])
    o_ref[...] = _dot3(s, w_ref[...]) + b_ref[...]


def _modulation(cond, mod_w, mod_b):
    nl, d, n = mod_w.shape
    rows = cond.shape[0]
    tn = d
    return pl.pallas_call(
        _mod_kernel,
        grid=(nl, n // tn),
        in_specs=[
            pl.BlockSpec((rows, d), lambda l, j: (0, 0)),
            pl.BlockSpec((None, d, tn), lambda l, j: (l, 0, j)),
            pl.BlockSpec((None, 1, tn), lambda l, j: (l, 0, j)),
        ],
        out_specs=pl.BlockSpec((None, rows, tn), lambda l, j: (l, 0, j)),
        out_shape=jax.ShapeDtypeStruct((nl, rows, n), F32),
        compiler_params=_params(("arbitrary", "arbitrary")),
        name="modulation",
    )(cond, mod_w, mod_b.reshape(nl, 1, n))


def _inproj_kernel(x_ref, mod_ref, g_ref, bm_ref, w_ref, msk_ref, cw_ref, cb_ref, wg_ref,
                   brg_ref, big_ref, lam_ref, pa_ref, lg_ref, pg_ref, a_ref, u_ref, *, na, nb):
    h = _rmsnorm(x_ref[...], g_ref[...])
    h = h * (1.0 + mod_ref[0, 1:2, :]) + mod_ref[0, 0:1, :]
    hb = h.astype(BF16)
    lru_w = nb // 2
    tm = hb.shape[0]
    xin = _dot(hb, w_ref[:, na:na + lru_w])
    lg_ref[...] = _gelu_tanh(_dot(hb, w_ref[:, na + lru_w:na + nb]))

    n_blocks = lru_w // LANES
    ng = pg_ref.shape[1]

    def wide_piece(ref, lo, hi, w_off, act):
        val = _dot(hb, w_ref[:, w_off + lo:w_off + hi])
        ref[:, lo:hi] = val if act is None else act(val, lo, hi)

    merge_act = lambda val, lo, hi: _sigmoid(val + bm_ref[:, lo:hi])
    pieces = ([(pa_ref, lo, hi, 0, None) for lo, hi in _lane_pieces(na, n_blocks * na // (na + ng))]
              + [(pg_ref, lo, hi, na + nb, merge_act)
                 for lo, hi in _lane_pieces(ng, n_blocks - n_blocks * na // (na + ng))])

    has_m1, has_m2, has_p1 = msk_ref[0, 0], msk_ref[0, 1], msk_ref[0, 2]
    sp = [_softplus(-lam_ref[d]) for d in range(N_DIR)]
    for n in range(n_blocks):
        wide_piece(*pieces[n])
        sl = slice(n * LANES, (n + 1) * LANES)
        x = xin[:, sl]
        xc = (pltpu.roll(x, 2, 0) * has_m2 * cw_ref[0:1, sl]
              + pltpu.roll(x, 1, 0) * has_m1 * cw_ref[1:2, sl] + x * cw_ref[2:3, sl]
              + pltpu.roll(x, tm - 1, 0) * has_p1 * cw_ref[3:4, sl] + cb_ref[:, sl])
        xcb = xc.astype(BF16)
        for d in range(N_DIR):
            gz = _dot(xcb, wg_ref[d, n])
            rg = _sigmoid(gz[:, :LANES] + brg_ref[d, :, sl])
            ig = _sigmoid(gz[:, LANES:] + big_ref[d, :, sl])
            log_a = (-LRU_C) * rg * sp[d][:, sl]
            a = jnp.exp(log_a)
            a_ref[d, :, sl] = a
            u_ref[d, :, sl] = jnp.sqrt(1.0 - a * a) * ig * xc
    for piece in pieces[n_blocks:]:
        wide_piece(*piece)


def _lane_pieces(width, n):
    tiles = width // LANES
    assert width % LANES == 0 and 1 <= n <= tiles
    bounds = [LANES * (tiles * i // n) for i in range(n + 1)]
    return list(zip(bounds[:-1], bounds[1:]))


def _inproj(x, mod_t, g, bm, w, masks, n_first_kind, cw, cb, wg, brg, big, lam, na, nb, tm):
    t, d = x.shape
    n = w.shape[1]
    ng = n - na - nb
    lru_w = nb // 2
    nblk = lru_w // LANES
    c2 = lambda i: (0, 0)
    c3 = lambda i: (0, 0, 0)
    dvec = pl.BlockSpec((N_DIR, 1, lru_w), c3)
    return pl.pallas_call(
        functools.partial(_inproj_kernel, na=na, nb=nb),
        grid=(t // tm,),
        in_specs=[
            pl.BlockSpec((tm, d), lambda i: (i, 0)),
            pl.BlockSpec((1, N_MOD, d), lambda i: (i, 0, 0)),
            pl.BlockSpec((1, d), c2),
            pl.BlockSpec((1, ng), c2),
            pl.BlockSpec((d, n), c2, pipeline_mode=pl.Buffered(1)),
            pl.BlockSpec((1, 3, tm, LANES), lambda i: (jnp.where(i < n_first_kind, 0, 1), 0, 0, 0)),
            pl.BlockSpec(cw.shape, c2),
            pl.BlockSpec((1, lru_w), c2),
            pl.BlockSpec((N_DIR, nblk, LANES, 2 * LANES), lambda i: (0, 0, 0, 0)),
            dvec, dvec, dvec,
        ],
        out_specs=[
            pl.BlockSpec((tm, na), lambda i: (i, 0)),
            pl.BlockSpec((tm, lru_w), lambda i: (i, 0)),
            pl.BlockSpec((tm, ng), lambda i: (i, 0)),
            pl.BlockSpec((N_DIR, tm, lru_w), lambda i: (0, i, 0)),
            pl.BlockSpec((N_DIR, tm, lru_w), lambda i: (0, i, 0)),
        ],
        out_shape=[
            jax.ShapeDtypeStruct((t, na), F32),
            jax.ShapeDtypeStruct((t, lru_w), F32),
            jax.ShapeDtypeStruct((t, ng), F32),
            jax.ShapeDtypeStruct((N_DIR, t, lru_w), F32),
            jax.ShapeDtypeStruct((N_DIR, t, lru_w), F32),
        ],
        compiler_params=_params(("arbitrary",)),
        name="inproj",
    )(x, mod_t, g, bm, w, masks, cw, cb, wg, brg, big, lam)


def _conv_tap_masks(tm, row_lens):
    out = np.zeros((len(row_lens), 3, tm, LANES), np.float32)
    for kind, rl in enumerate(row_lens):
        assert tm % rl == 0
        pos = np.arange(tm) % rl
        out[kind, 0] = (pos >= 1)[:, None]
        out[kind, 1] = (pos >= 2)[:, None]
        out[kind, 2] = (pos <= rl - 2)[:, None]
    return jnp.asarray(out)


T_FWD, T_BWD, T_FIRST, T_LAST, T_SEQ = 0, 1, 2, 3, 4
T_FWD_PREV, T_FWD_NEXT, T_BWD_PREV, T_BWD_NEXT = 5, 6, 7, 8


def _scan_table(seq_lens, row_lens):
    rows = [[] for _ in range(9)]
    base = 0
    for s, (ls, rl) in enumerate(zip(seq_lens, row_lens)):
        assert ls % CHUNK == 0 and rl % CHUNK == 0 and ls % rl == 0
        nc = ls // CHUNK
        for n in range(nc):
            cf, cb = n, nc - 1 - n
            rows[T_FWD].append(base + cf)
            rows[T_BWD].append(base + cb)
            rows[T_FIRST].append(int(n == 0))
            rows[T_LAST].append(int(n == nc - 1))
            rows[T_SEQ].append(s)
            rows[T_FWD_PREV].append(int((cf * CHUNK) % rl != 0))
            rows[T_FWD_NEXT].append(int(((cf + 1) * CHUNK) % rl != 0))
            rows[T_BWD_PREV].append(int((cb * CHUNK) % rl != 0))
            rows[T_BWD_NEXT].append(int(((cb + 1) * CHUNK) % rl != 0))
        base += nc
    return jnp.asarray(np.array(rows, dtype=np.int32))


def _chunk_map(which):
    return lambda g, tab: (tab[which, g], 0)


def _halo_prev_map(which):
    return lambda g, tab: (jnp.maximum(tab[which, g] * (CHUNK // SUBLANES) - 1, 0), 0)


def _halo_next_map(which, n_blocks8):
    return lambda g, tab: (
        jnp.minimum((tab[which, g] + 1) * (CHUNK // SUBLANES), n_blocks8 - 1), 0)


def _rwkv_prologue(pa_ref, halo_ref, halo_ok, d, mu_ref, w0_ref, wup_ref, a0_ref, aup_ref,
                   kk_ref, ka_ref, rk_ref, e_ref, et_ref, bonus_ref, *, width, n_shift):
    reverse = bool(d)
    row = lax.broadcasted_iota(jnp.int32, (CHUNK, 1), 0)
    if reverse:
        edge, halo_row, shift = row == CHUNK - 1, 0, CHUNK - 1
    else:
        edge, halo_row, shift = row == 0, SUBLANES - 1, 1

    def mix(lo, hi):
        ps = pa_ref[:, lo:hi]
        nb = halo_ref[halo_row:halo_row + 1, lo:hi] * halo_ok
        shifted = jnp.where(edge, nb, pltpu.roll(ps, shift, 0))
        return ps + (shifted - ps) * mu_ref[d, :, lo:hi]

    r = mix(0, width)
    k = mix(width, 2 * width)
    v = mix(2 * width, 3 * width)
    wa = mix(3 * width, n_shift)

    lw = w0_ref[d] + _dot(jnp.tanh(wa).astype(BF16), wup_ref[d])
    logw = -jnp.exp(-_softplus(-lw) - 0.5)
    a = _sigmoid(a0_ref[d] + _dot(wa.astype(BF16), aup_ref[d]))

    kk = k * kk_ref[...]
    nrm = jnp.sqrt(_seg_sum(kk * kk, e_ref, et_ref))
    kk = kk / jnp.maximum(nrm, 1e-12)
    k = k * (1.0 + (a - 1.0) * ka_ref[...])
    bonus_ref[...] = _seg_sum(r * k * rk_ref[...], e_ref, et_ref) * v

    ti = lax.broadcasted_iota(jnp.int32, (CHUNK, CHUNK), 0)
    si = lax.broadcasted_iota(jnp.int32, (CHUNK, CHUNK), 1)
    tri = jnp.where((si >= ti) if reverse else (si <= ti), 1.0, 0.0).astype(BF16)
    gc = _sel_dot(tri, logw)
    g_end = gc[0:1, :] if reverse else gc[CHUNK - 1:CHUNK, :]

    beta = kk * a
    inv = jnp.exp(-gc)
    to_end = jnp.exp(g_end - gc)
    return dict(
        v=v, r_t=r * jnp.exp(gc), a_t=-kk * jnp.exp(gc - logw), b_t=beta * inv, k_t=k * inv,
        b_h=beta * to_end, k_h=k * to_end, dec_end=jnp.exp(g_end))


def _rwkv_kernel(tab_ref, paf_ref, halof_ref, pab_ref, halob_ref, s0_ref, mu_ref, w0_ref,
                 wup_ref, a0_ref, aup_ref, kk_ref, ka_ref, rk_ref, e_ref, et_ref, rep_ref,
                 rept_ref, yf_ref, bonf_ref, yb_ref, bonb_ref, sfin_ref, s_scr, *, width,
                 n_shift):
    g = pl.program_id(0)
    n_groups = width // GROUP
    cat = jnp.concatenate
    same_head = (lax.broadcasted_iota(jnp.int32, (GROUP, GROUP), 0) // HEAD_DIM
                 == lax.broadcasted_iota(jnp.int32, (GROUP, GROUP), 1) // HEAD_DIM)

    @pl.when(tab_ref[T_FIRST, g] == 1)
    def _():
        for d in range(N_DIR):
            for q in range(n_groups):
                tiled = _dot_sel(s0_ref[0, d, q * GROUP:(q + 1) * GROUP, :], rep_ref[...])
                s_scr[d, q] = jnp.where(same_head, tiled, 0.0)

    shared = (mu_ref, w0_ref, wup_ref, a0_ref, aup_ref, kk_ref, ka_ref, rk_ref, e_ref, et_ref)
    pro = [
        _rwkv_prologue(paf_ref, halof_ref, tab_ref[T_FWD_PREV, g].astype(F32), 0, *shared,
                       bonf_ref, width=width, n_shift=n_shift),
        _rwkv_prologue(pab_ref, halob_ref, tab_ref[T_BWD_NEXT, g].astype(F32), 1, *shared,
                       bonb_ref, width=width, n_shift=n_shift),
    ]
    y_refs = (yf_ref, yb_ref)

    lane_head = lax.broadcasted_iota(jnp.int32, (1, GROUP), 1) // HEAD_DIM
    head_masks = [lane_head == h for h in range(GROUP // HEAD_DIM)]

    def bd(x):
        return cat([jnp.where(m, x, 0.0) for m in head_masks], axis=0).astype(BF16)

    t_row = lax.broadcasted_iota(jnp.int32, (CHUNK, GROUP), 0)
    s_lane = lax.broadcasted_iota(jnp.int32, (CHUNK, GROUP), 1) % CHUNK
    strict = (s_lane < t_row, s_lane > t_row)
    incl = (s_lane <= t_row, s_lane >= t_row)
    eye = jnp.where(s_lane == t_row, 1.0, 0.0)

    items = [(d, q, slice(q * GROUP, (q + 1) * GROUP))
             for d in range(N_DIR) for q in range(n_groups)]
    n = range(len(items))

    lhs_ar = [cat([pro[d]["a_t"][:, s], pro[d]["r_t"][:, s]], axis=0).astype(BF16)
              for d, q, s in items]
    v_bd = [bd(pro[d]["v"][:, s]) for d, q, s in items]
    aa = [_dot_nt(lhs_ar[i], cat([bd(pro[d]["b_t"][:, s]), bd(pro[d]["k_t"][:, s])], axis=0))
          for i, (d, q, s) in enumerate(items)]
    a_ab = [jnp.where(strict[d], aa[i][:CHUNK, :GROUP], 0.0) for i, (d, q, s) in enumerate(items)]
    a_ak = [jnp.where(strict[d], aa[i][:CHUNK, GROUP:], 0.0).astype(BF16)
            for i, (d, q, s) in enumerate(items)]
    a_r = [cat([jnp.where(incl[d], aa[i][CHUNK:, :GROUP], 0.0),
                jnp.where(incl[d], aa[i][CHUNK:, GROUP:], 0.0)], axis=1).astype(BF16)
           for i, (d, q, s) in enumerate(items)]
    s_old = [s_scr[d, q] for d, q, s in items]
    s_lhs = [_dot_nt(lhs_ar[i], s_old[i].astype(BF16)) for i in n]
    rhs = [s_lhs[i][:CHUNK] + _dot(a_ak[i], v_bd[i]) for i in n]

    inv_m = [eye + x for x in a_ab]
    pw = [_dot(x.astype(BF16), bd(x)) for x in a_ab]
    for _ in range(CHUNK.bit_length() - 3):
        both = [_dot(cat([pw[i], inv_m[i]], axis=0).astype(BF16), bd(pw[i])) for i in n]
        pw = [x[:CHUNK] for x in both]
        inv_m = [inv_m[i] + both[i][CHUNK:] for i in n]
    inv_m = [inv_m[i] + _dot(inv_m[i].astype(BF16), bd(pw[i])) for i in n]

    u = [_dot(inv_m[i].astype(BF16), bd(rhs[i])) for i in n]
    for i, (d, q, s) in enumerate(items):
        y_refs[d][:, s] = s_lhs[i][CHUNK:] + _dot(a_r[i], cat([bd(u[i]), v_bd[i]], axis=0))
    for i, (d, q, s) in enumerate(items):
        upd = _dot_tn(cat([u[i], pro[d]["v"][:, s]], axis=0).astype(BF16),
                      cat([pro[d]["b_h"][:, s], pro[d]["k_h"][:, s]], axis=0).astype(BF16))
        s_scr[d, q] = s_old[i] * pro[d]["dec_end"][:, s] + jnp.where(same_head, upd, 0.0)

    @pl.when(tab_ref[T_LAST, g] == 1)
    def _():
        for d in range(N_DIR):
            for q in range(n_groups):
                sfin_ref[0, d, q * GROUP:(q + 1) * GROUP, :] = _dot_sel(s_scr[d, q],
                                                                          rept_ref[...])


def _rwkv_scan(tab, pa, s0, mu, w0, wup, a0, aup, k_k, k_a, r_k, e, et, *, width, n_shift):
    t, na = pa.shape
    n_steps = tab.shape[1]
    n_seq = s0.shape[0]
    n_groups = width // GROUP
    rep = np.tile(np.eye(HEAD_DIM, dtype=np.float32), (1, GROUP // HEAD_DIM))
    c2 = lambda g, tab: (0, 0)
    c3 = lambda g, tab: (0, 0, 0)
    vec = pl.BlockSpec((1, width), c2)
    dvec = pl.BlockSpec((N_DIR, 1, width), c3)
    dmat = pl.BlockSpec((N_DIR, LANES, width), c3)
    state_spec = pl.BlockSpec((1, N_DIR, width, HEAD_DIM), lambda g, tab: (tab[T_SEQ, g], 0, 0, 0))
    fwd_out = pl.BlockSpec((CHUNK, width), _chunk_map(T_FWD))
    bwd_out = pl.BlockSpec((CHUNK, width), _chunk_map(T_BWD))
    grid_spec = pltpu.PrefetchScalarGridSpec(
        num_scalar_prefetch=1,
        grid=(n_steps,),
        in_specs=[
            pl.BlockSpec((CHUNK, na), _chunk_map(T_FWD)),
            pl.BlockSpec((SUBLANES, na), _halo_prev_map(T_FWD)),
            pl.BlockSpec((CHUNK, na), _chunk_map(T_BWD)),
            pl.BlockSpec((SUBLANES, na), _halo_next_map(T_BWD, t // SUBLANES)),
            state_spec,
            pl.BlockSpec((N_DIR, 1, n_shift), c3),
            dvec, dmat, dvec, dmat,
            vec, vec, vec,
            pl.BlockSpec((width, LANES), c2),
            pl.BlockSpec((LANES, width), c2),
            pl.BlockSpec((HEAD_DIM, GROUP), c2),
            pl.BlockSpec((GROUP, HEAD_DIM), c2),
        ],
        out_specs=[fwd_out, fwd_out, bwd_out, bwd_out, state_spec],
        scratch_shapes=[pltpu.VMEM((N_DIR, n_groups, GROUP, GROUP), F32)],
    )
    tok = jax.ShapeDtypeStruct((t, width), F32)
    return pl.pallas_call(
        functools.partial(_rwkv_kernel, width=width, n_shift=n_shift),
        grid_spec=grid_spec,
        out_shape=[tok, tok, tok, tok,
                   jax.ShapeDtypeStruct((n_seq, N_DIR, width, HEAD_DIM), F32)],
        compiler_params=_params(("arbitrary",)),
        name="rwkv_scan",
    )(tab, pa, pa, pa, pa, s0, mu, w0, wup, a0, aup, k_k, k_a, r_k, e, et,
      jnp.asarray(rep, BF16), jnp.asarray(rep.T.copy(), BF16))


def _lru_kernel(tab_ref, af_ref, uf_ref, ab_ref, ub_ref, h0_ref, hf_ref, hb_ref, hfin_ref, h_scr):
    g = pl.program_id(0)

    @pl.when(tab_ref[T_FIRST, g] == 1)
    def _():
        h_scr[...] = h0_ref[0]

    hf = h_scr[0:1, :]
    hb = h_scr[1:2, :]
    for i in range(CHUNK):
        tf, tb = i, CHUNK - 1 - i
        hf = af_ref[tf:tf + 1, :] * hf + uf_ref[tf:tf + 1, :]
        hf_ref[tf:tf + 1, :] = hf
        hb = ab_ref[tb:tb + 1, :] * hb + ub_ref[tb:tb + 1, :]
        hb_ref[tb:tb + 1, :] = hb
    h_scr[0:1, :] = hf
    h_scr[1:2, :] = hb

    @pl.when(tab_ref[T_LAST, g] == 1)
    def _():
        hfin_ref[0] = h_scr[...]


def _lru_scan(tab, a, u, h0):
    _, t, width = a.shape
    n_steps = tab.shape[1]
    n_seq = h0.shape[0]
    fwd_in = pl.BlockSpec((None, CHUNK, width), lambda g, tab: (0, tab[T_FWD, g], 0))
    bwd_in = pl.BlockSpec((None, CHUNK, width), lambda g, tab: (1, tab[T_BWD, g], 0))
    state_spec = pl.BlockSpec((1, N_DIR, width), lambda g, tab: (tab[T_SEQ, g], 0, 0))
    grid_spec = pltpu.PrefetchScalarGridSpec(
        num_scalar_prefetch=1,
        grid=(n_steps,),
        in_specs=[fwd_in, fwd_in, bwd_in, bwd_in, state_spec],
        out_specs=[
            pl.BlockSpec((CHUNK, width), _chunk_map(T_FWD)),
            pl.BlockSpec((CHUNK, width), _chunk_map(T_BWD)),
            state_spec,
        ],
        scratch_shapes=[pltpu.VMEM((N_DIR, width), F32)],
    )
    return pl.pallas_call(
        _lru_kernel,
        grid_spec=grid_spec,
        out_shape=[
            jax.ShapeDtypeStruct((t, width), F32),
            jax.ShapeDtypeStruct((t, width), F32),
            jax.ShapeDtypeStruct((n_seq, N_DIR, width), F32),
        ],
        compiler_params=_params(("arbitrary",)),
        name="lru_scan",
    )(tab, a, u, a, u, h0)


def _postmix_kernel(yf_ref, yb_ref, bf_ref, bb_ref, gd_ref, hf_ref, hb_ref, lg_ref, mg_ref,
                    x_ref, mod_ref, lnw_ref, lnb_ref, gup_ref, wa_ref, wb_ref, wo_ref,
                    gpost_ref, gpre_ref, e_ref, et_ref, *rest, d_model, n_experts):
    if n_experts:
        router_ref, x1_ref, h2_ref, gates_ref = rest
    else:
        x1_ref, h2_ref = rest
    y = yf_ref[...] + yb_ref[...]
    inv_d = 1.0 / HEAD_DIM
    mean = _seg_sum(y, e_ref, et_ref) * inv_d
    yc = y - mean
    var = _seg_sum(yc * yc, e_ref, et_ref) * inv_d
    yn = yc * lax.rsqrt(var + LNX_EPS) * lnw_ref[...] + lnb_ref[...]
    gate = _dot(_sigmoid(gd_ref[...]).astype(BF16), gup_ref[...])
    ya = (yn + bf_ref[...] + bb_ref[...]) * gate
    yl = (hf_ref[...] + hb_ref[...]) * lg_ref[...]
    proj_a = _dot(ya.astype(BF16), wa_ref[...])
    proj_b = _dot(yl.astype(BF16), wb_ref[...])
    m = mg_ref[:, :d_model] * proj_a + mg_ref[:, d_model:] * proj_b
    o = _dot(m.astype(BF16), wo_ref[...])
    x1 = x_ref[...] + mod_ref[0, 2:3, :] * _rmsnorm(o, gpost_ref[...])
    x1_ref[...] = x1
    h2 = _rmsnorm(x1, gpre_ref[...]) * (1.0 + mod_ref[0, 4:5, :]) + mod_ref[0, 3:4, :]
    h2_ref[...] = h2.astype(BF16)
    if n_experts:
        lane = lax.broadcasted_iota(jnp.int32, (1, LANES), 1).astype(F32)
        logits = jnp.where(lane < n_experts, _dot3(h2, router_ref[...]), -jnp.inf)
        m1 = jnp.max(logits, axis=-1, keepdims=True)
        i1 = jnp.min(jnp.where(logits == m1, lane, float(LANES)), axis=-1, keepdims=True)
        rest_l = jnp.where(lane == i1, -jnp.inf, logits)
        m2 = jnp.max(rest_l, axis=-1, keepdims=True)
        i2 = jnp.min(jnp.where(rest_l == m2, lane, float(LANES)), axis=-1, keepdims=True)
        e2 = jnp.exp(m2 - m1)
        w1 = 1.0 / (1.0 + e2)
        gates = jnp.where(lane == i1, w1, 0.0) + jnp.where(lane == i2, e2 * w1, 0.0)
        gates_ref[...] = gates.T[:SUBLANES]


def _postmix(yf, yb, bf, bb, pa, hf, hb, pb, pg, x, mod_t, lnw, lnb, gup, wa, wb, wo, gpost,
             gpre, e, et, router, *, tm, gd_block):
    t, d = x.shape
    width = yf.shape[1]
    lru_w = hf.shape[1]
    n_experts = 0 if router is None else router.shape[1]
    row = lambda i: (i, 0)
    const = lambda i: (0, 0)
    vec_d = pl.BlockSpec((1, d), const)
    vec_w = pl.BlockSpec((1, width), const)
    tile_w = pl.BlockSpec((tm, width), row)
    tile_l = pl.BlockSpec((tm, lru_w), row)
    in_specs = [
        tile_w, tile_w, tile_w, tile_w,
        pl.BlockSpec((tm, LANES), lambda i: (i, gd_block)),
        tile_l, tile_l, tile_l,
        pl.BlockSpec((tm, 2 * d), row),
        pl.BlockSpec((tm, d), row),
        pl.BlockSpec((1, N_MOD, d), lambda i: (i, 0, 0)),
        vec_w, vec_w,
        pl.BlockSpec((LANES, width), const),
        pl.BlockSpec((width, d), const),
        pl.BlockSpec((lru_w, d), const),
        pl.BlockSpec((d, d), const),
        vec_d, vec_d,
        pl.BlockSpec((width, LANES), const),
        pl.BlockSpec((LANES, width), const),
    ]
    args = [yf, yb, bf, bb, pa, hf, hb, pb, pg, x, mod_t, lnw, lnb, gup, wa, wb, wo, gpost,
            gpre, e, et]
    out_specs = [pl.BlockSpec((tm, d), row), pl.BlockSpec((tm, d), row)]
    out_shape = [jax.ShapeDtypeStruct((t, d), F32), jax.ShapeDtypeStruct((t, d), BF16)]
    if n_experts:
        assert n_experts <= SUBLANES
        router_pad = jnp.zeros((d, LANES), F32).at[:, :n_experts].set(router)
        in_specs.append(pl.BlockSpec((d, LANES), const))
        args.append(router_pad)
        out_specs.append(pl.BlockSpec((SUBLANES, tm), lambda i: (0, i)))
        out_shape.append(jax.ShapeDtypeStruct((SUBLANES, t), F32))
    return pl.pallas_call(
        functools.partial(_postmix_kernel, d_model=d, n_experts=n_experts),
        grid=(t // tm,),
        in_specs=in_specs,
        out_specs=out_specs,
        out_shape=out_shape,
        compiler_params=_params(("arbitrary",)),
        name="postmix",
    )(*args)


def _ffn_kernel(h_ref, x1_ref, mod_ref, gpost_ref, w1_ref, w3_ref, w2_ref, o_ref, acc_ref):
    f = pl.program_id(1)

    @pl.when(f == 0)
    def _():
        acc_ref[...] = jnp.zeros_like(acc_ref)

    h = h_ref[...]
    hid = _silu(_dot(h, w1_ref[...])) * _dot(h, w3_ref[...])
    acc_ref[...] += _dot(hid.astype(BF16), w2_ref[...])

    @pl.when(f == pl.num_programs(1) - 1)
    def _():
        o_ref[...] = x1_ref[...] + mod_ref[0, 5:6, :] * _rmsnorm(acc_ref[...], gpost_ref[...])


def _ffn(h2, x1, mod_t, gpost, w1, w3, w2, *, tm, tf):
    t, d = x1.shape
    ff = w1.shape[1]
    row = lambda i, f: (i, 0)
    return pl.pallas_call(
        _ffn_kernel,
        grid=(t // tm, ff // tf),
        in_specs=[
            pl.BlockSpec((tm, d), row),
            pl.BlockSpec((tm, d), row),
            pl.BlockSpec((1, N_MOD, d), lambda i, f: (i, 0, 0)),
            pl.BlockSpec((1, d), lambda i, f: (0, 0)),
            pl.BlockSpec((d, tf), lambda i, f: (0, f)),
            pl.BlockSpec((d, tf), lambda i, f: (0, f)),
            pl.BlockSpec((tf, d), lambda i, f: (f, 0)),
        ],
        out_specs=pl.BlockSpec((tm, d), row),
        out_shape=jax.ShapeDtypeStruct((t, d), F32),
        scratch_shapes=[pltpu.VMEM((tm, d), F32)],
        compiler_params=_params(("arbitrary", "arbitrary")),
        name="ffn_dense",
    )(h2, x1, mod_t, gpost, w1, w3, w2)


def _moe_kernel(gt_ref, before_ref, h_ref, x1_ref, mod_ref, gpost_ref, w1_ref, w3_ref, w2_ref,
                o_ref, pos_scr, x_scr, y_scr):
    e = pl.program_id(1)
    f = pl.program_id(2)
    last_f = pl.num_programs(2) - 1

    @pl.when((e == 0) & (f == 0))
    def _():
        chosen_all = jnp.where(gt_ref[...] > 0.0, 1.0, 0.0).astype(BF16)
        pos_scr[...] = _dot(chosen_all, before_ref[...])
        o_ref[...] = jnp.zeros_like(o_ref)

    is_e = lax.broadcasted_iota(jnp.int32, (SUBLANES, 1), 0) == e
    gate_row = jnp.sum(jnp.where(is_e, gt_ref[...], 0.0), axis=0, keepdims=True)
    pos_row = jnp.sum(jnp.where(is_e, pos_scr[...], 0.0), axis=0, keepdims=True)
    chosen = gate_row > 0.0
    slot_row = jnp.where(chosen, pos_row, -1.0)
    count = jnp.sum(jnp.where(chosen, 1.0, 0.0)).astype(jnp.int32)
    n_sub = (count + (MOE_SUB - 1)) // MOE_SUB

    def body(j, carry):
        r0 = pl.multiple_of(j * MOE_SUB, MOE_SUB)
        slots = (r0 + lax.broadcasted_iota(jnp.int32, (MOE_SUB, 1), 0)).astype(F32)
        onehot = slot_row == slots
        p = jnp.where(onehot, 1.0, 0.0).astype(BF16)

        @pl.when(f == 0)
        def _():
            x_scr[pl.ds(r0, MOE_SUB), :] = _dot(p, h_ref[...]).astype(BF16)

        x = x_scr[pl.ds(r0, MOE_SUB), :]
        hid = _silu(_dot(x, w1_ref[...])) * _dot(x, w3_ref[...])
        g_col = jnp.sum(jnp.where(onehot, gate_row, 0.0), axis=1, keepdims=True)
        part = _dot((hid * g_col).astype(BF16), w2_ref[...])

        @pl.when(f == 0)
        def _():
            y_scr[pl.ds(r0, MOE_SUB), :] = part

        @pl.when(f > 0)
        def _():
            y_scr[pl.ds(r0, MOE_SUB), :] += part

        @pl.when(f == last_f)
        def _():
            hi, lo = _split2(y_scr[pl.ds(r0, MOE_SUB), :])
            o_ref[...] += _dot_tn(jnp.concatenate([p, p], axis=0),
                                  jnp.concatenate([hi, lo], axis=0))
        return carry

    lax.fori_loop(0, n_sub, body, 0)

    @pl.when((e == pl.num_programs(1) - 1) & (f == last_f))
    def _():
        o_ref[...] = x1_ref[...] + mod_ref[0, 5:6, :] * _rmsnorm(o_ref[...], gpost_ref[...])


def _moe(gates_t, h2, x1, mod_t, gpost, w1, w3, w2, *, tb, tf):
    t, d = x1.shape
    n_e, _, ff = w1.shape
    before = jnp.asarray(np.triu(np.ones((tb, tb), np.float32), 1), BF16)
    row = lambda b, e, f: (b, 0)
    return pl.pallas_call(
        _moe_kernel,
        grid=(t // tb, n_e, ff // tf),
        in_specs=[
            pl.BlockSpec((SUBLANES, tb), lambda b, e, f: (0, b)),
            pl.BlockSpec((tb, tb), lambda b, e, f: (0, 0)),
            pl.BlockSpec((tb, d), row),
            pl.BlockSpec((tb, d), row),
            pl.BlockSpec((1, N_MOD, d), lambda b, e, f: (b, 0, 0)),
            pl.BlockSpec((1, d), lambda b, e, f: (0, 0)),
            pl.BlockSpec((None, d, tf), lambda b, e, f: (e, 0, f)),
            pl.BlockSpec((None, d, tf), lambda b, e, f: (e, 0, f)),
            pl.BlockSpec((None, tf, d), lambda b, e, f: (e, f, 0)),
        ],
        out_specs=pl.BlockSpec((tb, d), row),
        out_shape=jax.ShapeDtypeStruct((t, d), F32),
        scratch_shapes=[pltpu.VMEM((SUBLANES, tb), F32), pltpu.VMEM((tb, d), BF16),
                        pltpu.VMEM((tb, d), F32)],
        compiler_params=_params(("arbitrary", "arbitrary", "arbitrary")),
        name="ffn_moe",
    )(gates_t, before, h2, x1, mod_t, gpost, w1, w3, w2)


def _head_indicator(width):
    e = np.zeros((width, LANES), np.float32)
    e[np.arange(width), np.arange(width) // HEAD_DIM] = 1.0
    return jnp.asarray(e, BF16), jnp.asarray(e.T.copy(), BF16)


def _tile_mod(mod_l, tile_rows, tm, d):
    idx = np.asarray(tile_rows[::tm], np.int32)
    return jnp.take(mod_l, idx, axis=0).reshape(len(idx), N_MOD, d)


def _pick_tile(t, pref):
    while t % pref:
        pref //= 2
    return pref


def kernel(x_prompt, x_sample, c, c_ctx, state_rwkv, state_lru, mod_w, mod_b, norm_pre_mix, norm_post_mix, norm_pre_ffn, norm_post_ffn, w_in, b_merge, rwkv_mu, rwkv_w0, rwkv_w_up, rwkv_a0, rwkv_a_up, rwkv_k_k, rwkv_k_a, rwkv_r_k, rwkv_g_up, rwkv_lnx_w, rwkv_lnx_b, lru_conv_w, lru_conv_b, lru_w_rg, lru_b_rg, lru_w_ig, lru_b_ig, lru_lam, w_proj_a, w_proj_b, w_out, ffn_w1, ffn_w3, ffn_w2, moe_router, moe_w1, moe_w3, moe_w2):
    b_p, seq, d = x_prompt.shape
    b_s, dec_seq, _ = x_sample.shape
    depth = w_in.shape[0]
    n_heads, head_dim = rwkv_r_k.shape[1], rwkv_r_k.shape[2]
    width = n_heads * head_dim
    n_shift = rwkv_mu.shape[2]
    lru_w = lru_lam.shape[2]
    blk = lru_w_rg.shape[3]
    gate_lora = rwkv_g_up.shape[1]
    lora = rwkv_w_up.shape[2]
    na = n_shift + gate_lora
    nb = 2 * lru_w
    assert head_dim == HEAD_DIM and blk == LANES and rwkv_mu.shape[1] == N_DIR
    assert n_shift == 3 * width + 2 * lora and 2 * lora == LANES and gate_lora == LANES
    assert dec_seq % GRID_W == 0 and lru_conv_w.shape[1] == 4 and width % GROUP == 0

    t_p, t_s = b_p * seq, b_s * dec_seq
    x = jnp.concatenate([x_prompt.reshape(t_p, d), x_sample.reshape(t_s, d)], axis=0)

    n_cond = -(-(1 + b_s) // 16) * 16
    cond = jnp.zeros((n_cond, d), F32).at[0].set(c_ctx).at[1:1 + b_s].set(c)
    mod = _modulation(cond, mod_w, mod_b)
    tile_rows = np.concatenate([np.zeros(t_p, np.int32),
                                1 + np.arange(t_s, dtype=np.int32) // dec_seq])

    tab = _scan_table([seq] * b_p + [dec_seq] * b_s, [seq] * b_p + [GRID_W] * b_s)
    e_ind, et_ind = _head_indicator(width)

    tm = _pick_tile(np.gcd(seq, dec_seq), 256)
    tap_masks = _conv_tap_masks(tm, [seq, GRID_W])
    tm_ffn = _pick_tile(np.gcd(t_p, dec_seq), 512)
    tb_moe = _pick_tile(np.gcd(t_p, dec_seq), 1024)

    rwkv_fin, lru_fin = [], []
    for l in range(depth):
        mod_t = _tile_mod(mod[l], tile_rows, tm, d)
        wg = jnp.concatenate([lru_w_rg[l], lru_w_ig[l]], axis=-1).astype(BF16)
        pa, lgate, pg, lru_a, lru_u = _inproj(
            x, mod_t, norm_pre_mix[l][None], b_merge[l][None], w_in[l].astype(BF16),
            tap_masks, t_p // tm, lru_conv_w[l], lru_conv_b[l][None], wg, lru_b_rg[l][:, None],
            lru_b_ig[l][:, None], lru_lam[l][:, None], na, nb, tm)

        s0 = jnp.concatenate(
            [jnp.zeros((b_p, N_DIR, width, HEAD_DIM), F32),
             state_rwkv[:, l].astype(F32).reshape(b_s, N_DIR, width, HEAD_DIM)], axis=0)
        zpad = jnp.zeros((N_DIR, lora, width), F32)
        wup = jnp.concatenate([rwkv_w_up[l], zpad], axis=1).astype(BF16)
        aup = jnp.concatenate([zpad, rwkv_a_up[l]], axis=1).astype(BF16)
        yf, bonf, yb, bonb, sfin = _rwkv_scan(
            tab, pa, s0, rwkv_mu[l][:, None], rwkv_w0[l][:, None], wup, rwkv_a0[l][:, None], aup,
            rwkv_k_k[l][None], rwkv_k_a[l][None], rwkv_r_k[l].reshape(1, width), e_ind, et_ind,
            width=width, n_shift=n_shift)
        rwkv_fin.append(sfin[:b_p].reshape(b_p, N_DIR, n_heads, HEAD_DIM, HEAD_DIM))

        h0 = jnp.concatenate([jnp.zeros((b_p, N_DIR, lru_w), F32),
                              state_lru[:, l].astype(F32)], axis=0)
        hf, hb, hfin = _lru_scan(tab, lru_a, lru_u, h0)
        lru_fin.append(hfin[:b_p])

        is_moe = l % 2 == 1
        i = l // 2
        post = _postmix(
            yf, yb, bonf, bonb, pa, hf, hb, lgate, pg, x, mod_t,
            rwkv_lnx_w[l][None], rwkv_lnx_b[l][None], rwkv_g_up[l].astype(BF16),
            w_proj_a[l].astype(BF16), w_proj_b[l].astype(BF16), w_out[l].astype(BF16),
            norm_post_mix[l][None], norm_pre_ffn[l][None], e_ind, et_ind,
            moe_router[i] if is_moe else None, tm=tm, gd_block=n_shift // LANES)
        if is_moe:
            x1, h2, gates_t = post
            ff = moe_w1.shape[3]
            x = _moe(gates_t, h2, x1, _tile_mod(mod[l], tile_rows, tb_moe, d),
                     norm_post_ffn[l][None], moe_w1[i].astype(BF16), moe_w3[i].astype(BF16),
                     moe_w2[i].astype(BF16), tb=tb_moe,
                     tf=ff // 2 if ff % (2 * LANES) == 0 else ff)
        else:
            x1, h2 = post
            ff = ffn_w1.shape[2]
            x = _ffn(h2, x1, _tile_mod(mod[l], tile_rows, tm_ffn, d), norm_post_ffn[l][None],
                     ffn_w1[i].astype(BF16), ffn_w3[i].astype(BF16), ffn_w2[i].astype(BF16),
                     tm=tm_ffn, tf=_pick_tile(ff, 1024))

    y_prompt = x[:t_p].reshape(b_p, seq, d).astype(x_prompt.dtype)
    y_sample = x[t_p:].reshape(b_s, dec_seq, d).astype(x_sample.dtype)
    new_state_rwkv = jnp.stack(rwkv_fin, axis=1).astype(x_prompt.dtype)
    new_state_lru = jnp.stack(lru_fin, axis=1).astype(x_prompt.dtype)
    return (y_prompt, y_sample, new_state_rwkv, new_state_lru)
```

```python
import functools

import numpy as np
import jax
import jax.numpy as jnp
from jax import lax
from jax.experimental import pallas as pl
from jax.experimental.pallas import tpu as pltpu

F32 = jnp.float32
BF16 = jnp.bfloat16

GRID_W = 64
CHUNK = 64
LANES = 128
SUBLANES = 8
HEAD_DIM = 64
GROUP = 256
MOE_SUB = 128
LRU_C = 8.0
EPS = 1e-6
LNX_EPS = 64e-5
N_MOD = 6
N_DIR = 2
VMEM_LIMIT = 56 * 1024 * 1024


def _dot(a, b):
    return jnp.dot(a, b, preferred_element_type=F32)


def _dot_nt(a, b):
    return lax.dot_general(a, b, (((1,), (1,)), ((), ())), preferred_element_type=F32)


def _dot_tn(a, b):
    return lax.dot_general(a, b, (((0,), (0,)), ((), ())), preferred_element_type=F32)


def _split2(x):
    hi = x.astype(BF16)
    lo = (x - hi.astype(F32)).astype(BF16)
    return hi, lo


def _split3(x):
    p1 = x.astype(BF16)
    rem = x - p1.astype(F32)
    p2 = rem.astype(BF16)
    p3 = (rem - p2.astype(F32)).astype(BF16)
    return p1, p2, p3


def _dot3(a, b):
    ah, al = _split2(a)
    bh, bl = _split2(b)
    return _dot(ah, bh) + _dot(ah, bl) + _dot(al, bh)


def _sel_dot(sel, x):
    p1, p2, p3 = _split3(x)
    return _dot(sel, p1) + _dot(sel, p2) + _dot(sel, p3)


def _dot_sel(x, sel):
    p1, p2, p3 = _split3(x)
    return _dot(p1, sel) + _dot(p2, sel) + _dot(p3, sel)


def _sigmoid(x):
    return 0.5 * jnp.tanh(0.5 * x) + 0.5


def _softplus(x):
    return jnp.maximum(x, 0.0) + jnp.log(1.0 + jnp.exp(-jnp.abs(x)))


def _silu(x):
    return x * _sigmoid(x)


def _gelu_tanh(x):
    return 0.5 * x * (1.0 + jnp.tanh(0.7978845608028654 * (x + 0.044715 * (x * x * x))))


def _rmsnorm(x, g):
    ms = jnp.mean(x * x, axis=-1, keepdims=True)
    return x * lax.rsqrt(ms + EPS) * g


def _seg_sum(x, e_ref, et_ref):
    hi, lo = _split2(x)
    hs = _dot(hi, e_ref[...]) + _dot(lo, e_ref[...])
    hh, hl = _split2(hs)
    return _dot(hh, et_ref[...]) + _dot(hl, et_ref[...])


def _params(sem):
    return pltpu.CompilerParams(dimension_semantics=sem, vmem_limit_bytes=VMEM_LIMIT)


def _mod_kernel(c_ref, w_ref, b_ref, o_ref):
    s = _silu(c_ref[...])
    o_ref[...] = _dot3(s, w_ref[...]) + b_ref[...]


def _modulation(cond, mod_w, mod_b):
    nl, d, n = mod_w.shape
    rows = cond.shape[0]
    tn = d
    return pl.pallas_call(
        _mod_kernel,
        grid=(nl, n // tn),
        in_specs=[
            pl.BlockSpec((rows, d), lambda l, j: (0, 0)),
            pl.BlockSpec((None, d, tn), lambda l, j: (l, 0, j)),
            pl.BlockSpec((None, 1, tn), lambda l, j: (l, 0, j)),
        ],
        out_specs=pl.BlockSpec((None, rows, tn), lambda l, j: (l, 0, j)),
        out_shape=jax.ShapeDtypeStruct((nl, rows, n), F32),
        compiler_params=_params(("arbitrary", "arbitrary")),
        name="modulation",
    )(cond, mod_w, mod_b.reshape(nl, 1, n))


def _inproj_kernel(x_ref, mod_ref, g_ref, bm_ref, w_ref, msk_ref, cw_ref, cb_ref, wg_ref,
                   brg_ref, big_ref, lam_ref, pa_ref, lg_ref, pg_ref, a_ref, u_ref, *, na, nb):
    h = _rmsnorm(x_ref[...], g_ref[...])
    h = h * (1.0 + mod_ref[0, 1:2, :]) + mod_ref[0, 0:1, :]
    hb = h.astype(BF16)
    lru_w = nb // 2
    tm = hb.shape[0]
    xin = _dot(hb, w_ref[:, na:na + lru_w])
    lg_ref[...] = _gelu_tanh(_dot(hb, w_ref[:, na + lru_w:na + nb]))

    n_blocks = lru_w // LANES
    ng = pg_ref.shape[1]

    def wide_piece(ref, lo, hi, w_off, act):
        val = _dot(hb, w_ref[:, w_off + lo:w_off + hi])
        ref[:, lo:hi] = val if act is None else act(val, lo, hi)

    merge_act = lambda val, lo, hi: _sigmoid(val + bm_ref[:, lo:hi])
    pieces = ([(pa_ref, lo, hi, 0, None) for lo, hi in _lane_pieces(na, n_blocks * na // (na + ng))]
              + [(pg_ref, lo, hi, na + nb, merge_act)
                 for lo, hi in _lane_pieces(ng, n_blocks - n_blocks * na // (na + ng))])

    has_m1, has_m2, has_p1 = msk_ref[0, 0], msk_ref[0, 1], msk_ref[0, 2]
    sp = [_softplus(-lam_ref[d]) for d in range(N_DIR)]
    for n in range(n_blocks):
        wide_piece(*pieces[n])
        sl = slice(n * LANES, (n + 1) * LANES)
        x = xin[:, sl]
        xc = (pltpu.roll(x, 2, 0) * has_m2 * cw_ref[0:1, sl]
              + pltpu.roll(x, 1, 0) * has_m1 * cw_ref[1:2, sl] + x * cw_ref[2:3, sl]
              + pltpu.roll(x, tm - 1, 0) * has_p1 * cw_ref[3:4, sl] + cb_ref[:, sl])
        xcb = xc.astype(BF16)
        for d in range(N_DIR):
            gz = _dot(xcb, wg_ref[d, n])
            rg = _sigmoid(gz[:, :LANES] + brg_ref[d, :, sl])
            ig = _sigmoid(gz[:, LANES:] + big_ref[d, :, sl])
            log_a = (-LRU_C) * rg * sp[d][:, sl]
            a = jnp.exp(log_a)
            a_ref[d, :, sl] = a
            u_ref[d, :, sl] = jnp.sqrt(1.0 - a * a) * ig * xc
    for piece in pieces[n_blocks:]:
        wide_piece(*piece)


def _lane_pieces(width, n):
    tiles = width // LANES
    assert width % LANES == 0 and 1 <= n <= tiles
    bounds = [LANES * (tiles * i // n) for i in range(n + 1)]
    return list(zip(bounds[:-1], bounds[1:]))


def _inproj(x, mod_t, g, bm, w, masks, n_first_kind, cw, cb, wg, brg, big, lam, na, nb, tm):
    t, d = x.shape
    n = w.shape[1]
    ng = n - na - nb
    lru_w = nb // 2
    nblk = lru_w // LANES
    c2 = lambda i: (0, 0)
    c3 = lambda i: (0, 0, 0)
    dvec = pl.BlockSpec((N_DIR, 1, lru_w), c3)
    return pl.pallas_call(
        functools.partial(_inproj_kernel, na=na, nb=nb),
        grid=(t // tm,),
        in_specs=[
            pl.BlockSpec((tm, d), lambda i: (i, 0)),
            pl.BlockSpec((1, N_MOD, d), lambda i: (i, 0, 0)),
            pl.BlockSpec((1, d), c2),
            pl.BlockSpec((1, ng), c2),
            pl.BlockSpec((d, n), c2, pipeline_mode=pl.Buffered(1)),
            pl.BlockSpec((1, 3, tm, LANES), lambda i: (jnp.where(i < n_first_kind, 0, 1), 0, 0, 0)),
            pl.BlockSpec(cw.shape, c2),
            pl.BlockSpec((1, lru_w), c2),
            pl.BlockSpec((N_DIR, nblk, LANES, 2 * LANES), lambda i: (0, 0, 0, 0)),
            dvec, dvec, dvec,
        ],
        out_specs=[
            pl.BlockSpec((tm, na), lambda i: (i, 0)),
            pl.BlockSpec((tm, lru_w), lambda i: (i, 0)),
            pl.BlockSpec((tm, ng), lambda i: (i, 0)),
            pl.BlockSpec((N_DIR, tm, lru_w), lambda i: (0, i, 0)),
            pl.BlockSpec((N_DIR, tm, lru_w), lambda i: (0, i, 0)),
        ],
        out_shape=[
            jax.ShapeDtypeStruct((t, na), F32),
            jax.ShapeDtypeStruct((t, lru_w), F32),
            jax.ShapeDtypeStruct((t, ng), F32),
            jax.ShapeDtypeStruct((N_DIR, t, lru_w), F32),
            jax.ShapeDtypeStruct((N_DIR, t, lru_w), F32),
        ],
        compiler_params=_params(("arbitrary",)),
        name="inproj",
    )(x, mod_t, g, bm, w, masks, cw, cb, wg, brg, big, lam)


def _conv_tap_masks(tm, row_lens):
    out = np.zeros((len(row_lens), 3, tm, LANES), np.float32)
    for kind, rl in enumerate(row_lens):
        assert tm % rl == 0
        pos = np.arange(tm) % rl
        out[kind, 0] = (pos >= 1)[:, None]
        out[kind, 1] = (pos >= 2)[:, None]
        out[kind, 2] = (pos <= rl - 2)[:, None]
    return jnp.asarray(out)


T_FWD, T_BWD, T_FIRST, T_LAST, T_SEQ = 0, 1, 2, 3, 4
T_FWD_PREV, T_FWD_NEXT, T_BWD_PREV, T_BWD_NEXT = 5, 6, 7, 8


def _scan_table(seq_lens, row_lens, chunk):
    rows = [[] for _ in range(9)]
    base = 0
    for s, (ls, rl) in enumerate(zip(seq_lens, row_lens)):
        assert ls % chunk == 0 and ls % rl == 0 and (rl % chunk == 0 or chunk % rl == 0)
        nc = ls // chunk
        for n in range(nc):
            cf, cb = n, nc - 1 - n
            rows[T_FWD].append(base + cf)
            rows[T_BWD].append(base + cb)
            rows[T_FIRST].append(int(n == 0))
            rows[T_LAST].append(int(n == nc - 1))
            rows[T_SEQ].append(s)
            rows[T_FWD_PREV].append(int((cf * chunk) % rl != 0))
            rows[T_FWD_NEXT].append(int(((cf + 1) * chunk) % rl != 0))
            rows[T_BWD_PREV].append(int((cb * chunk) % rl != 0))
            rows[T_BWD_NEXT].append(int(((cb + 1) * chunk) % rl != 0))
        base += nc
    return jnp.asarray(np.array(rows, dtype=np.int32))


def _chunk_map(which):
    return lambda g, tab: (tab[which, g], 0)


def _halo_prev_map(which):
    return lambda g, tab: (jnp.maximum(tab[which, g] * (CHUNK // SUBLANES) - 1, 0), 0)


def _halo_next_map(which, n_blocks8):
    return lambda g, tab: (
        jnp.minimum((tab[which, g] + 1) * (CHUNK // SUBLANES), n_blocks8 - 1), 0)


def _rwkv_prologue(pa_ref, halo_ref, halo_ok, d, mu_ref, w0_ref, wup_ref, a0_ref, aup_ref,
                   kk_ref, ka_ref, rk_ref, e_ref, et_ref, bonus_ref, *, width, n_shift):
    reverse = bool(d)
    row = lax.broadcasted_iota(jnp.int32, (CHUNK, 1), 0)
    if reverse:
        edge, halo_row, shift = row == CHUNK - 1, 0, CHUNK - 1
    else:
        edge, halo_row, shift = row == 0, SUBLANES - 1, 1

    def mix(lo, hi):
        ps = pa_ref[:, lo:hi]
        nb = halo_ref[halo_row:halo_row + 1, lo:hi] * halo_ok
        shifted = jnp.where(edge, nb, pltpu.roll(ps, shift, 0))
        return ps + (shifted - ps) * mu_ref[d, :, lo:hi]

    r = mix(0, width)
    k = mix(width, 2 * width)
    v = mix(2 * width, 3 * width)
    wa = mix(3 * width, n_shift)

    lw = w0_ref[d] + _dot(jnp.tanh(wa).astype(BF16), wup_ref[d])
    logw = -jnp.exp(-_softplus(-lw) - 0.5)
    a = _sigmoid(a0_ref[d] + _dot(wa.astype(BF16), aup_ref[d]))

    kk = k * kk_ref[...]
    nrm = jnp.sqrt(_seg_sum(kk * kk, e_ref, et_ref))
    kk = kk / jnp.maximum(nrm, 1e-12)
    k = k * (1.0 + (a - 1.0) * ka_ref[...])
    bonus_ref[...] = _seg_sum(r * k * rk_ref[...], e_ref, et_ref) * v

    ti = lax.broadcasted_iota(jnp.int32, (CHUNK, CHUNK), 0)
    si = lax.broadcasted_iota(jnp.int32, (CHUNK, CHUNK), 1)
    tri = jnp.where((si >= ti) if reverse else (si <= ti), 1.0, 0.0).astype(BF16)
    gc = _sel_dot(tri, logw)
    g_end = gc[0:1, :] if reverse else gc[CHUNK - 1:CHUNK, :]

    beta = kk * a
    inv = jnp.exp(-gc)
    to_end = jnp.exp(g_end - gc)
    return dict(
        v=v, r_t=r * jnp.exp(gc), a_t=-kk * jnp.exp(gc - logw), b_t=beta * inv, k_t=k * inv,
        b_h=beta * to_end, k_h=k * to_end, dec_end=jnp.exp(g_end))


def _rwkv_kernel(tab_ref, paf_ref, halof_ref, pab_ref, halob_ref, s0_ref, mu_ref, w0_ref,
                 wup_ref, a0_ref, aup_ref, kk_ref, ka_ref, rk_ref, e_ref, et_ref, rep_ref,
                 rept_ref, yf_ref, bonf_ref, yb_ref, bonb_ref, sfin_ref, s_scr, *, width,
                 n_shift):
    g = pl.program_id(0)
    n_groups = width // GROUP
    cat = jnp.concatenate
    same_head = (lax.broadcasted_iota(jnp.int32, (GROUP, GROUP), 0) // HEAD_DIM
                 == lax.broadcasted_iota(jnp.int32, (GROUP, GROUP), 1) // HEAD_DIM)

    @pl.when(tab_ref[T_FIRST, g] == 1)
    def _():
        for d in range(N_DIR):
            for q in range(n_groups):
                tiled = _dot_sel(s0_ref[0, d, q * GROUP:(q + 1) * GROUP, :], rep_ref[...])
                s_scr[d, q] = jnp.where(same_head, tiled, 0.0)

    shared = (mu_ref, w0_ref, wup_ref, a0_ref, aup_ref, kk_ref, ka_ref, rk_ref, e_ref, et_ref)
    pro = [
        _rwkv_prologue(paf_ref, halof_ref, tab_ref[T_FWD_PREV, g].astype(F32), 0, *shared,
                       bonf_ref, width=width, n_shift=n_shift),
        _rwkv_prologue(pab_ref, halob_ref, tab_ref[T_BWD_NEXT, g].astype(F32), 1, *shared,
                       bonb_ref, width=width, n_shift=n_shift),
    ]
    y_refs = (yf_ref, yb_ref)

    lane_head = lax.broadcasted_iota(jnp.int32, (1, GROUP), 1) // HEAD_DIM
    head_masks = [lane_head == h for h in range(GROUP // HEAD_DIM)]

    def bd(x):
        return cat([jnp.where(m, x, 0.0) for m in head_masks], axis=0).astype(BF16)

    t_row = lax.broadcasted_iota(jnp.int32, (CHUNK, GROUP), 0)
    s_lane = lax.broadcasted_iota(jnp.int32, (CHUNK, GROUP), 1) % CHUNK
    strict = (s_lane < t_row, s_lane > t_row)
    incl = (s_lane <= t_row, s_lane >= t_row)
    eye = jnp.where(s_lane == t_row, 1.0, 0.0)

    items = [(d, q, slice(q * GROUP, (q + 1) * GROUP))
             for d in range(N_DIR) for q in range(n_groups)]
    n = range(len(items))

    lhs_ar = [cat([pro[d]["a_t"][:, s], pro[d]["r_t"][:, s]], axis=0).astype(BF16)
              for d, q, s in items]
    v_bd = [bd(pro[d]["v"][:, s]) for d, q, s in items]
    aa = [_dot_nt(lhs_ar[i], cat([bd(pro[d]["b_t"][:, s]), bd(pro[d]["k_t"][:, s])], axis=0))
          for i, (d, q, s) in enumerate(items)]
    a_ab = [jnp.where(strict[d], aa[i][:CHUNK, :GROUP], 0.0) for i, (d, q, s) in enumerate(items)]
    a_ak = [jnp.where(strict[d], aa[i][:CHUNK, GROUP:], 0.0).astype(BF16)
            for i, (d, q, s) in enumerate(items)]
    a_r = [cat([jnp.where(incl[d], aa[i][CHUNK:, :GROUP], 0.0),
                jnp.where(incl[d], aa[i][CHUNK:, GROUP:], 0.0)], axis=1).astype(BF16)
           for i, (d, q, s) in enumerate(items)]
    s_old = [s_scr[d, q] for d, q, s in items]
    s_lhs = [_dot_nt(lhs_ar[i], s_old[i].astype(BF16)) for i in n]
    rhs = [s_lhs[i][:CHUNK] + _dot(a_ak[i], v_bd[i]) for i in n]

    inv_m = [eye + x for x in a_ab]
    pw = [_dot(x.astype(BF16), bd(x)) for x in a_ab]
    for _ in range(CHUNK.bit_length() - 3):
        both = [_dot(cat([pw[i], inv_m[i]], axis=0).astype(BF16), bd(pw[i])) for i in n]
        pw = [x[:CHUNK] for x in both]
        inv_m = [inv_m[i] + both[i][CHUNK:] for i in n]
    inv_m = [inv_m[i] + _dot(inv_m[i].astype(BF16), bd(pw[i])) for i in n]

    u = [_dot(inv_m[i].astype(BF16), bd(rhs[i])) for i in n]
    for i, (d, q, s) in enumerate(items):
        y_refs[d][:, s] = s_lhs[i][CHUNK:] + _dot(a_r[i], cat([bd(u[i]), v_bd[i]], axis=0))
    for i, (d, q, s) in enumerate(items):
        upd = _dot_tn(cat([u[i], pro[d]["v"][:, s]], axis=0).astype(BF16),
                      cat([pro[d]["b_h"][:, s], pro[d]["k_h"][:, s]], axis=0).astype(BF16))
        s_scr[d, q] = s_old[i] * pro[d]["dec_end"][:, s] + jnp.where(same_head, upd, 0.0)

    @pl.when(tab_ref[T_LAST, g] == 1)
    def _():
        for d in range(N_DIR):
            for q in range(n_groups):
                sfin_ref[0, d, q * GROUP:(q + 1) * GROUP, :] = _dot_sel(s_scr[d, q],
                                                                          rept_ref[...])


def _rwkv_scan(tab, pa, s0, mu, w0, wup, a0, aup, k_k, k_a, r_k, e, et, *, width, n_shift):
    t, na = pa.shape
    n_steps = tab.shape[1]
    n_seq = s0.shape[0]
    n_groups = width // GROUP
    rep = np.tile(np.eye(HEAD_DIM, dtype=np.float32), (1, GROUP // HEAD_DIM))
    c2 = lambda g, tab: (0, 0)
    c3 = lambda g, tab: (0, 0, 0)
    vec = pl.BlockSpec((1, width), c2)
    dvec = pl.BlockSpec((N_DIR, 1, width), c3)
    dmat = pl.BlockSpec((N_DIR, LANES, width), c3)
    state_spec = pl.BlockSpec((1, N_DIR, width, HEAD_DIM), lambda g, tab: (tab[T_SEQ, g], 0, 0, 0))
    fwd_out = pl.BlockSpec((CHUNK, width), _chunk_map(T_FWD))
    bwd_out = pl.BlockSpec((CHUNK, width), _chunk_map(T_BWD))
    grid_spec = pltpu.PrefetchScalarGridSpec(
        num_scalar_prefetch=1,
        grid=(n_steps,),
        in_specs=[
            pl.BlockSpec((CHUNK, na), _chunk_map(T_FWD)),
            pl.BlockSpec((SUBLANES, na), _halo_prev_map(T_FWD)),
            pl.BlockSpec((CHUNK, na), _chunk_map(T_BWD)),
            pl.BlockSpec((SUBLANES, na), _halo_next_map(T_BWD, t // SUBLANES)),
            state_spec,
            pl.BlockSpec((N_DIR, 1, n_shift), c3),
            dvec, dmat, dvec, dmat,
            vec, vec, vec,
            pl.BlockSpec((width, LANES), c2),
            pl.BlockSpec((LANES, width), c2),
            pl.BlockSpec((HEAD_DIM, GROUP), c2),
            pl.BlockSpec((GROUP, HEAD_DIM), c2),
        ],
        out_specs=[fwd_out, fwd_out, bwd_out, bwd_out, state_spec],
        scratch_shapes=[pltpu.VMEM((N_DIR, n_groups, GROUP, GROUP), F32)],
    )
    tok = jax.ShapeDtypeStruct((t, width), F32)
    return pl.pallas_call(
        functools.partial(_rwkv_kernel, width=width, n_shift=n_shift),
        grid_spec=grid_spec,
        out_shape=[tok, tok, tok, tok,
                   jax.ShapeDtypeStruct((n_seq, N_DIR, width, HEAD_DIM), F32)],
        compiler_params=_params(("arbitrary",)),
        name="rwkv_scan",
    )(tab, pa, pa, pa, pa, s0, mu, w0, wup, a0, aup, k_k, k_a, r_k, e, et,
      jnp.asarray(rep, BF16), jnp.asarray(rep.T.copy(), BF16))


def _lru_kernel(tab_ref, af_ref, uf_ref, ab_ref, ub_ref, h0_ref, hf_ref, hb_ref, hfin_ref, h_scr):
    g = pl.program_id(0)

    @pl.when(tab_ref[T_FIRST, g] == 1)
    def _():
        h_scr[...] = h0_ref[0]

    hf = h_scr[0:1, :]
    hb = h_scr[1:2, :]
    rows = af_ref.shape[0]
    for i in range(rows):
        tf, tb = i, rows - 1 - i
        hf = af_ref[tf:tf + 1, :] * hf + uf_ref[tf:tf + 1, :]
        hf_ref[tf:tf + 1, :] = hf
        hb = ab_ref[tb:tb + 1, :] * hb + ub_ref[tb:tb + 1, :]
        hb_ref[tb:tb + 1, :] = hb
    h_scr[0:1, :] = hf
    h_scr[1:2, :] = hb

    @pl.when(tab_ref[T_LAST, g] == 1)
    def _():
        hfin_ref[0] = h_scr[...]


def _lru_scan(tab, a, u, h0, rows):
    _, t, width = a.shape
    n_steps = tab.shape[1]
    n_seq = h0.shape[0]
    fwd_in = pl.BlockSpec((None, rows, width), lambda g, tab: (0, tab[T_FWD, g], 0))
    bwd_in = pl.BlockSpec((None, rows, width), lambda g, tab: (1, tab[T_BWD, g], 0))
    state_spec = pl.BlockSpec((1, N_DIR, width), lambda g, tab: (tab[T_SEQ, g], 0, 0))
    grid_spec = pltpu.PrefetchScalarGridSpec(
        num_scalar_prefetch=1,
        grid=(n_steps,),
        in_specs=[fwd_in, fwd_in, bwd_in, bwd_in, state_spec],
        out_specs=[
            pl.BlockSpec((rows, width), _chunk_map(T_FWD)),
            pl.BlockSpec((rows, width), _chunk_map(T_BWD)),
            state_spec,
        ],
        scratch_shapes=[pltpu.VMEM((N_DIR, width), F32)],
    )
    return pl.pallas_call(
        _lru_kernel,
        grid_spec=grid_spec,
        out_shape=[
            jax.ShapeDtypeStruct((t, width), F32),
            jax.ShapeDtypeStruct((t, width), F32),
            jax.ShapeDtypeStruct((n_seq, N_DIR, width), F32),
        ],
        compiler_params=_params(("arbitrary",)),
        name="lru_scan",
    )(tab, a, u, a, u, h0)


def _postmix_kernel(yf_ref, yb_ref, bf_ref, bb_ref, gd_ref, hf_ref, hb_ref, lg_ref, mg_ref,
                    x_ref, mod_ref, lnw_ref, lnb_ref, gup_ref, wa_ref, wb_ref, wo_ref,
                    gpost_ref, gpre_ref, e_ref, et_ref, *rest, d_model, n_experts):
    if n_experts:
        router_ref, x1_ref, h2_ref, gates_ref = rest
    else:
        x1_ref, h2_ref = rest
    y = yf_ref[...] + yb_ref[...]
    inv_d = 1.0 / HEAD_DIM
    mean = _seg_sum(y, e_ref, et_ref) * inv_d
    yc = y - mean
    var = _seg_sum(yc * yc, e_ref, et_ref) * inv_d
    yn = yc * lax.rsqrt(var + LNX_EPS) * lnw_ref[...] + lnb_ref[...]
    gate = _dot(_sigmoid(gd_ref[...]).astype(BF16), gup_ref[...])
    ya = (yn + bf_ref[...] + bb_ref[...]) * gate
    yl = (hf_ref[...] + hb_ref[...]) * lg_ref[...]
    proj_a = _dot(ya.astype(BF16), wa_ref[...])
    proj_b = _dot(yl.astype(BF16), wb_ref[...])
    m = mg_ref[:, :d_model] * proj_a + mg_ref[:, d_model:] * proj_b
    o = _dot(m.astype(BF16), wo_ref[...])
    x1 = x_ref[...] + mod_ref[0, 2:3, :] * _rmsnorm(o, gpost_ref[...])
    x1_ref[...] = x1
    h2 = _rmsnorm(x1, gpre_ref[...]) * (1.0 + mod_ref[0, 4:5, :]) + mod_ref[0, 3:4, :]
    h2_ref[...] = h2.astype(BF16)
    if n_experts:
        lane = lax.broadcasted_iota(jnp.int32, (1, LANES), 1).astype(F32)
        logits = jnp.where(lane < n_experts, _dot3(h2, router_ref[...]), -jnp.inf)
        m1 = jnp.max(logits, axis=-1, keepdims=True)
        i1 = jnp.min(jnp.where(logits == m1, lane, float(LANES)), axis=-1, keepdims=True)
        rest_l = jnp.where(lane == i1, -jnp.inf, logits)
        m2 = jnp.max(rest_l, axis=-1, keepdims=True)
        i2 = jnp.min(jnp.where(rest_l == m2, lane, float(LANES)), axis=-1, keepdims=True)
        e2 = jnp.exp(m2 - m1)
        w1 = 1.0 / (1.0 + e2)
        gates = jnp.where(lane == i1, w1, 0.0) + jnp.where(lane == i2, e2 * w1, 0.0)
        gates_ref[...] = gates.T[:SUBLANES]


def _postmix(yf, yb, bf, bb, pa, hf, hb, pb, pg, x, mod_t, lnw, lnb, gup, wa, wb, wo, gpost,
             gpre, e, et, router, *, tm, gd_block):
    t, d = x.shape
    width = yf.shape[1]
    lru_w = hf.shape[1]
    n_experts = 0 if router is None else router.shape[1]
    row = lambda i: (i, 0)
    const = lambda i: (0, 0)
    vec_d = pl.BlockSpec((1, d), const)
    vec_w = pl.BlockSpec((1, width), const)
    tile_w = pl.BlockSpec((tm, width), row)
    tile_l = pl.BlockSpec((tm, lru_w), row)
    in_specs = [
        tile_w, tile_w, tile_w, tile_w,
        pl.BlockSpec((tm, LANES), lambda i: (i, gd_block)),
        tile_l, tile_l, tile_l,
        pl.BlockSpec((tm, 2 * d), row),
        pl.BlockSpec((tm, d), row),
        pl.BlockSpec((1, N_MOD, d), lambda i: (i, 0, 0)),
        vec_w, vec_w,
        pl.BlockSpec((LANES, width), const),
        pl.BlockSpec((width, d), const),
        pl.BlockSpec((lru_w, d), const),
        pl.BlockSpec((d, d), const),
        vec_d, vec_d,
        pl.BlockSpec((width, LANES), const),
        pl.BlockSpec((LANES, width), const),
    ]
    args = [yf, yb, bf, bb, pa, hf, hb, pb, pg, x, mod_t, lnw, lnb, gup, wa, wb, wo, gpost,
            gpre, e, et]
    out_specs = [pl.BlockSpec((tm, d), row), pl.BlockSpec((tm, d), row)]
    out_shape = [jax.ShapeDtypeStruct((t, d), F32), jax.ShapeDtypeStruct((t, d), BF16)]
    if n_experts:
        assert n_experts <= SUBLANES
        router_pad = jnp.zeros((d, LANES), F32).at[:, :n_experts].set(router)
        in_specs.append(pl.BlockSpec((d, LANES), const))
        args.append(router_pad)
        out_specs.append(pl.BlockSpec((SUBLANES, tm), lambda i: (0, i)))
        out_shape.append(jax.ShapeDtypeStruct((SUBLANES, t), F32))
    return pl.pallas_call(
        functools.partial(_postmix_kernel, d_model=d, n_experts=n_experts),
        grid=(t // tm,),
        in_specs=in_specs,
        out_specs=out_specs,
        out_shape=out_shape,
        compiler_params=_params(("arbitrary",)),
        name="postmix",
    )(*args)


def _ffn_kernel(h_ref, x1_ref, mod_ref, gpost_ref, w1_ref, w3_ref, w2_ref, o_ref, acc_ref):
    f = pl.program_id(1)

    @pl.when(f == 0)
    def _():
        acc_ref[...] = jnp.zeros_like(acc_ref)

    h = h_ref[...]
    hid = _silu(_dot(h, w1_ref[...])) * _dot(h, w3_ref[...])
    acc_ref[...] += _dot(hid.astype(BF16), w2_ref[...])

    @pl.when(f == pl.num_programs(1) - 1)
    def _():
        o_ref[...] = x1_ref[...] + mod_ref[0, 5:6, :] * _rmsnorm(acc_ref[...], gpost_ref[...])


def _ffn(h2, x1, mod_t, gpost, w1, w3, w2, *, tm, tf):
    t, d = x1.shape
    ff = w1.shape[1]
    row = lambda i, f: (i, 0)
    return pl.pallas_call(
        _ffn_kernel,
        grid=(t // tm, ff // tf),
        in_specs=[
            pl.BlockSpec((tm, d), row),
            pl.BlockSpec((tm, d), row),
            pl.BlockSpec((1, N_MOD, d), lambda i, f: (i, 0, 0)),
            pl.BlockSpec((1, d), lambda i, f: (0, 0)),
            pl.BlockSpec((d, tf), lambda i, f: (0, f)),
            pl.BlockSpec((d, tf), lambda i, f: (0, f)),
            pl.BlockSpec((tf, d), lambda i, f: (f, 0)),
        ],
        out_specs=pl.BlockSpec((tm, d), row),
        out_shape=jax.ShapeDtypeStruct((t, d), F32),
        scratch_shapes=[pltpu.VMEM((tm, d), F32)],
        compiler_params=_params(("arbitrary", "arbitrary")),
        name="ffn_dense",
    )(h2, x1, mod_t, gpost, w1, w3, w2)


def _moe_kernel(gt_ref, before_ref, h_ref, x1_ref, mod_ref, gpost_ref, w1_ref, w3_ref, w2_ref,
                o_ref, pos_scr, x_scr, y_scr):
    e = pl.program_id(1)
    f = pl.program_id(2)
    last_f = pl.num_programs(2) - 1

    @pl.when((e == 0) & (f == 0))
    def _():
        chosen_all = jnp.where(gt_ref[...] > 0.0, 1.0, 0.0).astype(BF16)
        pos_scr[...] = _dot(chosen_all, before_ref[...])
        o_ref[...] = jnp.zeros_like(o_ref)

    is_e = lax.broadcasted_iota(jnp.int32, (SUBLANES, 1), 0) == e
    gate_row = jnp.sum(jnp.where(is_e, gt_ref[...], 0.0), axis=0, keepdims=True)
    pos_row = jnp.sum(jnp.where(is_e, pos_scr[...], 0.0), axis=0, keepdims=True)
    chosen = gate_row > 0.0
    slot_row = jnp.where(chosen, pos_row, -1.0)
    count = jnp.sum(jnp.where(chosen, 1.0, 0.0)).astype(jnp.int32)
    n_sub = (count + (MOE_SUB - 1)) // MOE_SUB

    def body(j, carry):
        r0 = pl.multiple_of(j * MOE_SUB, MOE_SUB)
        slots = (r0 + lax.broadcasted_iota(jnp.int32, (MOE_SUB, 1), 0)).astype(F32)
        onehot = slot_row == slots
        p = jnp.where(onehot, 1.0, 0.0).astype(BF16)

        @pl.when(f == 0)
        def _():
            x_scr[pl.ds(r0, MOE_SUB), :] = _dot(p, h_ref[...]).astype(BF16)

        x = x_scr[pl.ds(r0, MOE_SUB), :]
        hid = _silu(_dot(x, w1_ref[...])) * _dot(x, w3_ref[...])
        g_col = jnp.sum(jnp.where(onehot, gate_row, 0.0), axis=1, keepdims=True)
        part = _dot((hid * g_col).astype(BF16), w2_ref[...])

        @pl.when(f == 0)
        def _():
            y_scr[pl.ds(r0, MOE_SUB), :] = part

        @pl.when(f > 0)
        def _():
            y_scr[pl.ds(r0, MOE_SUB), :] += part

        @pl.when(f == last_f)
        def _():
            hi, lo = _split2(y_scr[pl.ds(r0, MOE_SUB), :])
            o_ref[...] += _dot_tn(jnp.concatenate([p, p], axis=0),
                                  jnp.concatenate([hi, lo], axis=0))
        return carry

    lax.fori_loop(0, n_sub, body, 0)

    @pl.when((e == pl.num_programs(1) - 1) & (f == last_f))
    def _():
        o_ref[...] = x1_ref[...] + mod_ref[0, 5:6, :] * _rmsnorm(o_ref[...], gpost_ref[...])


def _moe(gates_t, h2, x1, mod_t, gpost, w1, w3, w2, *, tb, tf):
    t, d = x1.shape
    n_e, _, ff = w1.shape
    before = jnp.asarray(np.triu(np.ones((tb, tb), np.float32), 1), BF16)
    row = lambda b, e, f: (b, 0)
    return pl.pallas_call(
        _moe_kernel,
        grid=(t // tb, n_e, ff // tf),
        in_specs=[
            pl.BlockSpec((SUBLANES, tb), lambda b, e, f: (0, b)),
            pl.BlockSpec((tb, tb), lambda b, e, f: (0, 0)),
            pl.BlockSpec((tb, d), row),
            pl.BlockSpec((tb, d), row),
            pl.BlockSpec((1, N_MOD, d), lambda b, e, f: (b, 0, 0)),
            pl.BlockSpec((1, d), lambda b, e, f: (0, 0)),
            pl.BlockSpec((None, d, tf), lambda b, e, f: (e, 0, f)),
            pl.BlockSpec((None, d, tf), lambda b, e, f: (e, 0, f)),
            pl.BlockSpec((None, tf, d), lambda b, e, f: (e, f, 0)),
        ],
        out_specs=pl.BlockSpec((tb, d), row),
        out_shape=jax.ShapeDtypeStruct((t, d), F32),
        scratch_shapes=[pltpu.VMEM((SUBLANES, tb), F32), pltpu.VMEM((tb, d), BF16),
                        pltpu.VMEM((tb, d), F32)],
        compiler_params=_params(("arbitrary", "arbitrary", "arbitrary")),
        name="ffn_moe",
    )(gates_t, before, h2, x1, mod_t, gpost, w1, w3, w2)


def _head_indicator(width):
    e = np.zeros((width, LANES), np.float32)
    e[np.arange(width), np.arange(width) // HEAD_DIM] = 1.0
    return jnp.asarray(e, BF16), jnp.asarray(e.T.copy(), BF16)


def _tile_mod(mod_l, tile_rows, tm, d):
    idx = np.asarray(tile_rows[::tm], np.int32)
    return jnp.take(mod_l, idx, axis=0).reshape(len(idx), N_MOD, d)


def _pick_tile(t, pref):
    while t % pref:
        pref //= 2
    return pref


def kernel(x_prompt, x_sample, c, c_ctx, state_rwkv, state_lru, mod_w, mod_b, norm_pre_mix, norm_post_mix, norm_pre_ffn, norm_post_ffn, w_in, b_merge, rwkv_mu, rwkv_w0, rwkv_w_up, rwkv_a0, rwkv_a_up, rwkv_k_k, rwkv_k_a, rwkv_r_k, rwkv_g_up, rwkv_lnx_w, rwkv_lnx_b, lru_conv_w, lru_conv_b, lru_w_rg, lru_b_rg, lru_w_ig, lru_b_ig, lru_lam, w_proj_a, w_proj_b, w_out, ffn_w1, ffn_w3, ffn_w2, moe_router, moe_w1, moe_w3, moe_w2):
    b_p, seq, d = x_prompt.shape
    b_s, dec_seq, _ = x_sample.shape
    depth = w_in.shape[0]
    n_heads, head_dim = rwkv_r_k.shape[1], rwkv_r_k.shape[2]
    width = n_heads * head_dim
    n_shift = rwkv_mu.shape[2]
    lru_w = lru_lam.shape[2]
    blk = lru_w_rg.shape[3]
    gate_lora = rwkv_g_up.shape[1]
    lora = rwkv_w_up.shape[2]
    na = n_shift + gate_lora
    nb = 2 * lru_w
    assert head_dim == HEAD_DIM and blk == LANES and rwkv_mu.shape[1] == N_DIR
    assert n_shift == 3 * width + 2 * lora and 2 * lora == LANES and gate_lora == LANES
    assert dec_seq % GRID_W == 0 and lru_conv_w.shape[1] == 4 and width % GROUP == 0

    t_p, t_s = b_p * seq, b_s * dec_seq
    x = jnp.concatenate([x_prompt.reshape(t_p, d), x_sample.reshape(t_s, d)], axis=0)

    n_cond = -(-(1 + b_s) // 16) * 16
    cond = jnp.zeros((n_cond, d), F32).at[0].set(c_ctx).at[1:1 + b_s].set(c)
    mod = _modulation(cond, mod_w, mod_b)
    tile_rows = np.concatenate([np.zeros(t_p, np.int32),
                                1 + np.arange(t_s, dtype=np.int32) // dec_seq])

    seq_lens, row_lens = [seq] * b_p + [dec_seq] * b_s, [seq] * b_p + [GRID_W] * b_s
    tab = _scan_table(seq_lens, row_lens, CHUNK)
    lru_rows = _pick_tile(np.gcd(seq, dec_seq), 256)
    lru_tab = _scan_table(seq_lens, row_lens, lru_rows)
    e_ind, et_ind = _head_indicator(width)

    tm = _pick_tile(np.gcd(seq, dec_seq), 256)
    tap_masks = _conv_tap_masks(tm, [seq, GRID_W])
    tm_ffn = _pick_tile(np.gcd(t_p, dec_seq), 512)
    tb_moe = _pick_tile(np.gcd(t_p, dec_seq), 1024)

    rwkv_fin, lru_fin = [], []
    for l in range(depth):
        mod_t = _tile_mod(mod[l], tile_rows, tm, d)
        wg = jnp.concatenate([lru_w_rg[l], lru_w_ig[l]], axis=-1).astype(BF16)
        pa, lgate, pg, lru_a, lru_u = _inproj(
            x, mod_t, norm_pre_mix[l][None], b_merge[l][None], w_in[l].astype(BF16),
            tap_masks, t_p // tm, lru_conv_w[l], lru_conv_b[l][None], wg, lru_b_rg[l][:, None],
            lru_b_ig[l][:, None], lru_lam[l][:, None], na, nb, tm)

        s0 = jnp.concatenate(
            [jnp.zeros((b_p, N_DIR, width, HEAD_DIM), F32),
             state_rwkv[:, l].astype(F32).reshape(b_s, N_DIR, width, HEAD_DIM)], axis=0)
        zpad = jnp.zeros((N_DIR, lora, width), F32)
        wup = jnp.concatenate([rwkv_w_up[l], zpad], axis=1).astype(BF16)
        aup = jnp.concatenate([zpad, rwkv_a_up[l]], axis=1).astype(BF16)
        yf, bonf, yb, bonb, sfin = _rwkv_scan(
            tab, pa, s0, rwkv_mu[l][:, None], rwkv_w0[l][:, None], wup, rwkv_a0[l][:, None], aup,
            rwkv_k_k[l][None], rwkv_k_a[l][None], rwkv_r_k[l].reshape(1, width), e_ind, et_ind,
            width=width, n_shift=n_shift)
        rwkv_fin.append(sfin[:b_p].reshape(b_p, N_DIR, n_heads, HEAD_DIM, HEAD_DIM))

        h0 = jnp.concatenate([jnp.zeros((b_p, N_DIR, lru_w), F32),
                              state_lru[:, l].astype(F32)], axis=0)
        hf, hb, hfin = _lru_scan(lru_tab, lru_a, lru_u, h0, lru_rows)
        lru_fin.append(hfin[:b_p])

        is_moe = l % 2 == 1
        i = l // 2
        post = _postmix(
            yf, yb, bonf, bonb, pa, hf, hb, lgate, pg, x, mod_t,
            rwkv_lnx_w[l][None], rwkv_lnx_b[l][None], rwkv_g_up[l].astype(BF16),
            w_proj_a[l].astype(BF16), w_proj_b[l].astype(BF16), w_out[l].astype(BF16),
            norm_post_mix[l][None], norm_pre_ffn[l][None], e_ind, et_ind,
            moe_router[i] if is_moe else None, tm=tm, gd_block=n_shift // LANES)
        if is_moe:
            x1, h2, gates_t = post
            ff = moe_w1.shape[3]
            x = _moe(gates_t, h2, x1, _tile_mod(mod[l], tile_rows, tb_moe, d),
                     norm_post_ffn[l][None], moe_w1[i].astype(BF16), moe_w3[i].astype(BF16),
                     moe_w2[i].astype(BF16), tb=tb_moe,
                     tf=ff // 2 if ff % (2 * LANES) == 0 else ff)
        else:
            x1, h2 = post
            ff = ffn_w1.shape[2]
            x = _ffn(h2, x1, _tile_mod(mod[l], tile_rows, tm_ffn, d), norm_post_ffn[l][None],
                     ffn_w1[i].astype(BF16), ffn_w3[i].astype(BF16), ffn_w2[i].astype(BF16),
                     tm=tm_ffn, tf=_pick_tile(ff, 1024))

    y_prompt = x[:t_p].reshape(b_p, seq, d).astype(x_prompt.dtype)
    y_sample = x[t_p:].reshape(b_s, dec_seq, d).astype(x_sample.dtype)
    new_state_rwkv = jnp.stack(rwkv_fin, axis=1).astype(x_prompt.dtype)
    new_state_lru = jnp.stack(lru_fin, axis=1).astype(x_prompt.dtype)
    return (y_prompt, y_sample, new_state_rwkv, new_state_lru)
```

```python
import functools

import numpy as np
import jax
import jax.numpy as jnp
from jax import lax
from jax.experimental import pallas as pl
from jax.experimental.pallas import tpu as pltpu

F32 = jnp.float32
BF16 = jnp.bfloat16

GRID_W = 64
CHUNK = 64
LANES = 128
SUBLANES = 8
HEAD_DIM = 64
GROUP = 256
MOE_SUB = 128
LRU_C = 8.0
EPS = 1e-6
LNX_EPS = 64e-5
N_MOD = 6
N_DIR = 2
VMEM_LIMIT = 56 * 1024 * 1024


def _dot(a, b):
    return jnp.dot(a, b, preferred_element_type=F32)


def _dot_nt(a, b):
    return lax.dot_general(a, b, (((1,), (1,)), ((), ())), preferred_element_type=F32)


def _dot_tn(a, b):
    return lax.dot_general(a, b, (((0,), (0,)), ((), ())), preferred_element_type=F32)


def _split2(x):
    hi = x.astype(BF16)
    lo = (x - hi.astype(F32)).astype(BF16)
    return hi, lo


def _split3(x):
    p1 = x.astype(BF16)
    rem = x - p1.astype(F32)
    p2 = rem.astype(BF16)
    p3 = (rem - p2.astype(F32)).astype(BF16)
    return p1, p2, p3


def _dot3(a, b):
    ah, al = _split2(a)
    bh, bl = _split2(b)
    return _dot(ah, bh) + _dot(ah, bl) + _dot(al, bh)


def _sel_dot(sel, x):
    p1, p2, p3 = _split3(x)
    return _dot(sel, p1) + _dot(sel, p2) + _dot(sel, p3)


def _dot_sel(x, sel):
    p1, p2, p3 = _split3(x)
    return _dot(p1, sel) + _dot(p2, sel) + _dot(p3, sel)


def _sigmoid(x):
    return 0.5 * jnp.tanh(0.5 * x) + 0.5


def _softplus(x):
    return jnp.maximum(x, 0.0) + jnp.log(1.0 + jnp.exp(-jnp.abs(x)))


def _silu(x):
    return x * _sigmoid(x)


def _gelu_tanh(x):
    return 0.5 * x * (1.0 + jnp.tanh(0.7978845608028654 * (x + 0.044715 * (x * x * x))))


def _rmsnorm(x, g):
    ms = jnp.mean(x * x, axis=-1, keepdims=True)
    return x * lax.rsqrt(ms + EPS) * g


def _seg_sum(x, e_ref, et_ref):
    hs = _dot(x.astype(BF16), e_ref[...])
    hh, hl = _split2(hs)
    return _dot(hh, et_ref[...]) + _dot(hl, et_ref[...])


def _params(sem):
    return pltpu.CompilerParams(dimension_semantics=sem, vmem_limit_bytes=VMEM_LIMIT)


def _mod_kernel(c_ref, w_ref, b_ref, o_ref):
    s = _silu(c_ref[...])
    o_ref[...] = _dot3(s, w_ref[...]) + b_ref[...]


def _modulation(cond, mod_w, mod_b):
    nl, d, n = mod_w.shape
    rows = cond.shape[0]
    tn = d
    return pl.pallas_call(
        _mod_kernel,
        grid=(nl, n // tn),
        in_specs=[
            pl.BlockSpec((rows, d), lambda l, j: (0, 0)),
            pl.BlockSpec((None, d, tn), lambda l, j: (l, 0, j)),
            pl.BlockSpec((None, 1, tn), lambda l, j: (l, 0, j)),
        ],
        out_specs=pl.BlockSpec((None, rows, tn), lambda l, j: (l, 0, j)),
        out_shape=jax.ShapeDtypeStruct((nl, rows, n), F32),
        compiler_params=_params(("arbitrary", "arbitrary")),
        name="modulation",
    )(cond, mod_w, mod_b.reshape(nl, 1, n))


def _inproj_kernel(x_ref, mod_ref, g_ref, bm_ref, w_ref, msk_ref, cw_ref, cb_ref, wg_ref,
                   brg_ref, big_ref, lam_ref, pa_ref, lg_ref, pg_ref, a_ref, u_ref, *, na, nb):
    h = _rmsnorm(x_ref[...], g_ref[...])
    h = h * (1.0 + mod_ref[0, 1:2, :]) + mod_ref[0, 0:1, :]
    hb = h.astype(BF16)
    lru_w = nb // 2
    tm = hb.shape[0]
    xin = _dot(hb, w_ref[:, na:na + lru_w])
    lg_ref[...] = _gelu_tanh(_dot(hb, w_ref[:, na + lru_w:na + nb]))

    n_blocks = lru_w // LANES
    ng = pg_ref.shape[1]

    def wide_piece(ref, lo, hi, w_off, act):
        val = _dot(hb, w_ref[:, w_off + lo:w_off + hi])
        ref[:, lo:hi] = val if act is None else act(val, lo, hi)

    merge_act = lambda val, lo, hi: _sigmoid(val + bm_ref[:, lo:hi])
    pieces = ([(pa_ref, lo, hi, 0, None) for lo, hi in _lane_pieces(na, n_blocks * na // (na + ng))]
              + [(pg_ref, lo, hi, na + nb, merge_act)
                 for lo, hi in _lane_pieces(ng, n_blocks - n_blocks * na // (na + ng))])

    has_m1, has_m2, has_p1 = msk_ref[0, 0], msk_ref[0, 1], msk_ref[0, 2]
    sp = [_softplus(-lam_ref[d]) for d in range(N_DIR)]
    for n in range(n_blocks):
        wide_piece(*pieces[n])
        sl = slice(n * LANES, (n + 1) * LANES)
        x = xin[:, sl]
        xc = (pltpu.roll(x, 2, 0) * has_m2 * cw_ref[0:1, sl]
              + pltpu.roll(x, 1, 0) * has_m1 * cw_ref[1:2, sl] + x * cw_ref[2:3, sl]
              + pltpu.roll(x, tm - 1, 0) * has_p1 * cw_ref[3:4, sl] + cb_ref[:, sl])
        xcb = xc.astype(BF16)
        for d in range(N_DIR):
            gz = _dot(xcb, wg_ref[d, n])
            rg = _sigmoid(gz[:, :LANES] + brg_ref[d, :, sl])
            ig = _sigmoid(gz[:, LANES:] + big_ref[d, :, sl])
            log_a = (-LRU_C) * rg * sp[d][:, sl]
            a = jnp.exp(log_a)
            a_ref[d, :, sl] = a
            u_ref[d, :, sl] = jnp.sqrt(1.0 - a * a) * ig * xc
    for piece in pieces[n_blocks:]:
        wide_piece(*piece)


def _lane_pieces(width, n):
    tiles = width // LANES
    assert width % LANES == 0 and 1 <= n <= tiles
    bounds = [LANES * (tiles * i // n) for i in range(n + 1)]
    return list(zip(bounds[:-1], bounds[1:]))


def _inproj(x, mod_t, g, bm, w, masks, n_first_kind, cw, cb, wg, brg, big, lam, na, nb, tm):
    t, d = x.shape
    n = w.shape[1]
    ng = n - na - nb
    lru_w = nb // 2
    nblk = lru_w // LANES
    c2 = lambda i: (0, 0)
    c3 = lambda i: (0, 0, 0)
    dvec = pl.BlockSpec((N_DIR, 1, lru_w), c3)
    return pl.pallas_call(
        functools.partial(_inproj_kernel, na=na, nb=nb),
        grid=(t // tm,),
        in_specs=[
            pl.BlockSpec((tm, d), lambda i: (i, 0)),
            pl.BlockSpec((1, N_MOD, d), lambda i: (i, 0, 0)),
            pl.BlockSpec((1, d), c2),
            pl.BlockSpec((1, ng), c2),
            pl.BlockSpec((d, n), c2, pipeline_mode=pl.Buffered(1)),
            pl.BlockSpec((1, 3, tm, LANES), lambda i: (jnp.where(i < n_first_kind, 0, 1), 0, 0, 0)),
            pl.BlockSpec(cw.shape, c2),
            pl.BlockSpec((1, lru_w), c2),
            pl.BlockSpec((N_DIR, nblk, LANES, 2 * LANES), lambda i: (0, 0, 0, 0)),
            dvec, dvec, dvec,
        ],
        out_specs=[
            pl.BlockSpec((tm, na), lambda i: (i, 0)),
            pl.BlockSpec((tm, lru_w), lambda i: (i, 0)),
            pl.BlockSpec((tm, ng), lambda i: (i, 0)),
            pl.BlockSpec((N_DIR, tm, lru_w), lambda i: (0, i, 0)),
            pl.BlockSpec((N_DIR, tm, lru_w), lambda i: (0, i, 0)),
        ],
        out_shape=[
            jax.ShapeDtypeStruct((t, na), F32),
            jax.ShapeDtypeStruct((t, lru_w), F32),
            jax.ShapeDtypeStruct((t, ng), F32),
            jax.ShapeDtypeStruct((N_DIR, t, lru_w), F32),
            jax.ShapeDtypeStruct((N_DIR, t, lru_w), F32),
        ],
        compiler_params=_params(("arbitrary",)),
        name="inproj",
    )(x, mod_t, g, bm, w, masks, cw, cb, wg, brg, big, lam)


def _conv_tap_masks(tm, row_lens):
    out = np.zeros((len(row_lens), 3, tm, LANES), np.float32)
    for kind, rl in enumerate(row_lens):
        assert tm % rl == 0
        pos = np.arange(tm) % rl
        out[kind, 0] = (pos >= 1)[:, None]
        out[kind, 1] = (pos >= 2)[:, None]
        out[kind, 2] = (pos <= rl - 2)[:, None]
    return jnp.asarray(out)


T_FWD, T_BWD, T_FIRST, T_LAST, T_SEQ = 0, 1, 2, 3, 4
T_FWD_PREV, T_FWD_NEXT, T_BWD_PREV, T_BWD_NEXT = 5, 6, 7, 8


def _scan_table(seq_lens, row_lens, chunk):
    rows = [[] for _ in range(9)]
    base = 0
    for s, (ls, rl) in enumerate(zip(seq_lens, row_lens)):
        assert ls % chunk == 0 and ls % rl == 0 and (rl % chunk == 0 or chunk % rl == 0)
        nc = ls // chunk
        for n in range(nc):
            cf, cb = n, nc - 1 - n
            rows[T_FWD].append(base + cf)
            rows[T_BWD].append(base + cb)
            rows[T_FIRST].append(int(n == 0))
            rows[T_LAST].append(int(n == nc - 1))
            rows[T_SEQ].append(s)
            rows[T_FWD_PREV].append(int((cf * chunk) % rl != 0))
            rows[T_FWD_NEXT].append(int(((cf + 1) * chunk) % rl != 0))
            rows[T_BWD_PREV].append(int((cb * chunk) % rl != 0))
            rows[T_BWD_NEXT].append(int(((cb + 1) * chunk) % rl != 0))
        base += nc
    return jnp.asarray(np.array(rows, dtype=np.int32))


def _chunk_map(which):
    return lambda g, tab: (tab[which, g], 0)


def _halo_prev_map(which):
    return lambda g, tab: (jnp.maximum(tab[which, g] * (CHUNK // SUBLANES) - 1, 0), 0)


def _halo_next_map(which, n_blocks8):
    return lambda g, tab: (
        jnp.minimum((tab[which, g] + 1) * (CHUNK // SUBLANES), n_blocks8 - 1), 0)


def _rwkv_prologue(pa_ref, halo_ref, halo_ok, d, mu_ref, w0_ref, wup_ref, a0_ref, aup_ref,
                   kk_ref, ka_ref, rk_ref, e_ref, et_ref, bonus_ref, *, width, n_shift):
    reverse = bool(d)
    row = lax.broadcasted_iota(jnp.int32, (CHUNK, 1), 0)
    if reverse:
        edge, halo_row, shift = row == CHUNK - 1, 0, CHUNK - 1
    else:
        edge, halo_row, shift = row == 0, SUBLANES - 1, 1

    def mix(lo, hi):
        ps = pa_ref[:, lo:hi]
        nb = halo_ref[halo_row:halo_row + 1, lo:hi] * halo_ok
        shifted = jnp.where(edge, nb, pltpu.roll(ps, shift, 0))
        return ps + (shifted - ps) * mu_ref[d, :, lo:hi]

    r = mix(0, width)
    k = mix(width, 2 * width)
    v = mix(2 * width, 3 * width)
    wa = mix(3 * width, n_shift)

    lw = w0_ref[d] + _dot(jnp.tanh(wa).astype(BF16), wup_ref[d])
    logw = -jnp.exp(-_softplus(-lw) - 0.5)
    a = _sigmoid(a0_ref[d] + _dot(wa.astype(BF16), aup_ref[d]))

    kk = k * kk_ref[...]
    nrm = jnp.sqrt(_seg_sum(kk * kk, e_ref, et_ref))
    kk = kk / jnp.maximum(nrm, 1e-12)
    k = k * (1.0 + (a - 1.0) * ka_ref[...])
    bonus_ref[...] = _seg_sum(r * k * rk_ref[...], e_ref, et_ref) * v

    ti = lax.broadcasted_iota(jnp.int32, (CHUNK, CHUNK), 0)
    si = lax.broadcasted_iota(jnp.int32, (CHUNK, CHUNK), 1)
    tri = jnp.where((si >= ti) if reverse else (si <= ti), 1.0, 0.0).astype(BF16)
    gc = _sel_dot(tri, logw)
    g_end = gc[0:1, :] if reverse else gc[CHUNK - 1:CHUNK, :]

    beta = kk * a
    inv = jnp.exp(-gc)
    dec_end = jnp.exp(g_end)
    to_end = dec_end * inv
    return dict(
        v=v, r_t=r * jnp.exp(gc), a_t=-kk * jnp.exp(gc - logw), b_t=beta * inv, k_t=k * inv,
        b_h=beta * to_end, k_h=k * to_end, dec_end=dec_end)


def _rwkv_kernel(tab_ref, paf_ref, halof_ref, pab_ref, halob_ref, s0_ref, mu_ref, w0_ref,
                 wup_ref, a0_ref, aup_ref, kk_ref, ka_ref, rk_ref, e_ref, et_ref, rep_ref,
                 rept_ref, yf_ref, bonf_ref, yb_ref, bonb_ref, sfin_ref, s_scr, *, width,
                 n_shift):
    g = pl.program_id(0)
    n_groups = width // GROUP
    cat = jnp.concatenate
    same_head = (lax.broadcasted_iota(jnp.int32, (GROUP, GROUP), 0) // HEAD_DIM
                 == lax.broadcasted_iota(jnp.int32, (GROUP, GROUP), 1) // HEAD_DIM)

    @pl.when(tab_ref[T_FIRST, g] == 1)
    def _():
        for d in range(N_DIR):
            for q in range(n_groups):
                tiled = _dot_sel(s0_ref[0, d, q * GROUP:(q + 1) * GROUP, :], rep_ref[...])
                s_scr[d, q] = jnp.where(same_head, tiled, 0.0)

    shared = (mu_ref, w0_ref, wup_ref, a0_ref, aup_ref, kk_ref, ka_ref, rk_ref, e_ref, et_ref)
    pro = [
        _rwkv_prologue(paf_ref, halof_ref, tab_ref[T_FWD_PREV, g].astype(F32), 0, *shared,
                       bonf_ref, width=width, n_shift=n_shift),
        _rwkv_prologue(pab_ref, halob_ref, tab_ref[T_BWD_NEXT, g].astype(F32), 1, *shared,
                       bonb_ref, width=width, n_shift=n_shift),
    ]
    y_refs = (yf_ref, yb_ref)

    lane_head = lax.broadcasted_iota(jnp.int32, (1, GROUP), 1) // HEAD_DIM
    head_masks = [lane_head == h for h in range(GROUP // HEAD_DIM)]

    def bd(x):
        return cat([jnp.where(m, x, 0.0) for m in head_masks], axis=0).astype(BF16)

    t_row = lax.broadcasted_iota(jnp.int32, (CHUNK, GROUP), 0)
    s_lane = lax.broadcasted_iota(jnp.int32, (CHUNK, GROUP), 1) % CHUNK
    strict = (s_lane < t_row, s_lane > t_row)
    incl = (s_lane <= t_row, s_lane >= t_row)
    eye = jnp.where(s_lane == t_row, 1.0, 0.0)

    items = [(d, q, slice(q * GROUP, (q + 1) * GROUP))
             for d in range(N_DIR) for q in range(n_groups)]
    n = range(len(items))

    lhs_ar = [cat([pro[d]["a_t"][:, s], pro[d]["r_t"][:, s]], axis=0).astype(BF16)
              for d, q, s in items]
    v_bd = [bd(pro[d]["v"][:, s]) for d, q, s in items]
    aa = [_dot_nt(lhs_ar[i], cat([bd(pro[d]["b_t"][:, s]), bd(pro[d]["k_t"][:, s])], axis=0))
          for i, (d, q, s) in enumerate(items)]
    a_ab = [jnp.where(strict[d], aa[i][:CHUNK, :GROUP], 0.0) for i, (d, q, s) in enumerate(items)]
    a_ak = [jnp.where(strict[d], aa[i][:CHUNK, GROUP:], 0.0).astype(BF16)
            for i, (d, q, s) in enumerate(items)]
    a_r = [cat([jnp.where(incl[d], aa[i][CHUNK:, :GROUP], 0.0),
                jnp.where(incl[d], aa[i][CHUNK:, GROUP:], 0.0)], axis=1).astype(BF16)
           for i, (d, q, s) in enumerate(items)]
    s_old = [s_scr[d, q] for d, q, s in items]
    s_lhs = [_dot_nt(lhs_ar[i], s_old[i].astype(BF16)) for i in n]
    rhs = [s_lhs[i][:CHUNK] + _dot(a_ak[i], v_bd[i]) for i in n]

    inv_m = [eye + x for x in a_ab]
    pw = [_dot(x.astype(BF16), bd(x)) for x in a_ab]
    for _ in range(CHUNK.bit_length() - 3):
        both = [_dot(cat([pw[i], inv_m[i]], axis=0).astype(BF16), bd(pw[i])) for i in n]
        pw = [x[:CHUNK] for x in both]
        inv_m = [inv_m[i] + both[i][CHUNK:] for i in n]
    inv_m = [inv_m[i] + _dot(inv_m[i].astype(BF16), bd(pw[i])) for i in n]

    u = [_dot(inv_m[i].astype(BF16), bd(rhs[i])) for i in n]
    for i, (d, q, s) in enumerate(items):
        y_refs[d][:, s] = s_lhs[i][CHUNK:] + _dot(a_r[i], cat([bd(u[i]), v_bd[i]], axis=0))
    for i, (d, q, s) in enumerate(items):
        upd = _dot_tn(cat([u[i], pro[d]["v"][:, s]], axis=0).astype(BF16),
                      cat([pro[d]["b_h"][:, s], pro[d]["k_h"][:, s]], axis=0).astype(BF16))
        s_scr[d, q] = s_old[i] * pro[d]["dec_end"][:, s] + jnp.where(same_head, upd, 0.0)

    @pl.when(tab_ref[T_LAST, g] == 1)
    def _():
        for d in range(N_DIR):
            for q in range(n_groups):
                sfin_ref[0, d, q * GROUP:(q + 1) * GROUP, :] = _dot_sel(s_scr[d, q],
                                                                          rept_ref[...])


def _rwkv_scan(tab, pa, s0, mu, w0, wup, a0, aup, k_k, k_a, r_k, e, et, *, width, n_shift):
    t, na = pa.shape
    n_steps = tab.shape[1]
    n_seq = s0.shape[0]
    n_groups = width // GROUP
    rep = np.tile(np.eye(HEAD_DIM, dtype=np.float32), (1, GROUP // HEAD_DIM))
    c2 = lambda g, tab: (0, 0)
    c3 = lambda g, tab: (0, 0, 0)
    vec = pl.BlockSpec((1, width), c2)
    dvec = pl.BlockSpec((N_DIR, 1, width), c3)
    dmat = pl.BlockSpec((N_DIR, LANES, width), c3)
    state_spec = pl.BlockSpec((1, N_DIR, width, HEAD_DIM), lambda g, tab: (tab[T_SEQ, g], 0, 0, 0))
    fwd_out = pl.BlockSpec((CHUNK, width), _chunk_map(T_FWD))
    bwd_out = pl.BlockSpec((CHUNK, width), _chunk_map(T_BWD))
    grid_spec = pltpu.PrefetchScalarGridSpec(
        num_scalar_prefetch=1,
        grid=(n_steps,),
        in_specs=[
            pl.BlockSpec((CHUNK, na), _chunk_map(T_FWD)),
            pl.BlockSpec((SUBLANES, na), _halo_prev_map(T_FWD)),
            pl.BlockSpec((CHUNK, na), _chunk_map(T_BWD)),
            pl.BlockSpec((SUBLANES, na), _halo_next_map(T_BWD, t // SUBLANES)),
            state_spec,
            pl.BlockSpec((N_DIR, 1, n_shift), c3),
            dvec, dmat, dvec, dmat,
            vec, vec, vec,
            pl.BlockSpec((width, LANES), c2),
            pl.BlockSpec((LANES, width), c2),
            pl.BlockSpec((HEAD_DIM, GROUP), c2),
            pl.BlockSpec((GROUP, HEAD_DIM), c2),
        ],
        out_specs=[fwd_out, fwd_out, bwd_out, bwd_out, state_spec],
        scratch_shapes=[pltpu.VMEM((N_DIR, n_groups, GROUP, GROUP), F32)],
    )
    tok = jax.ShapeDtypeStruct((t, width), F32)
    return pl.pallas_call(
        functools.partial(_rwkv_kernel, width=width, n_shift=n_shift),
        grid_spec=grid_spec,
        out_shape=[tok, tok, tok, tok,
                   jax.ShapeDtypeStruct((n_seq, N_DIR, width, HEAD_DIM), F32)],
        compiler_params=_params(("arbitrary",)),
        name="rwkv_scan",
    )(tab, pa, pa, pa, pa, s0, mu, w0, wup, a0, aup, k_k, k_a, r_k, e, et,
      jnp.asarray(rep, BF16), jnp.asarray(rep.T.copy(), BF16))


def _lru_kernel(tab_ref, af_ref, uf_ref, ab_ref, ub_ref, h0_ref, hf_ref, hb_ref, hfin_ref, h_scr):
    g = pl.program_id(0)

    @pl.when(tab_ref[T_FIRST, g] == 1)
    def _():
        h_scr[...] = h0_ref[0]

    hf = h_scr[0:1, :]
    hb = h_scr[1:2, :]
    rows = af_ref.shape[0]
    for i in range(rows):
        tf, tb = i, rows - 1 - i
        hf = af_ref[tf:tf + 1, :] * hf + uf_ref[tf:tf + 1, :]
        hf_ref[tf:tf + 1, :] = hf
        hb = ab_ref[tb:tb + 1, :] * hb + ub_ref[tb:tb + 1, :]
        hb_ref[tb:tb + 1, :] = hb
    h_scr[0:1, :] = hf
    h_scr[1:2, :] = hb

    @pl.when(tab_ref[T_LAST, g] == 1)
    def _():
        hfin_ref[0] = h_scr[...]


def _lru_scan(tab, a, u, h0, rows):
    _, t, width = a.shape
    n_steps = tab.shape[1]
    n_seq = h0.shape[0]
    fwd_in = pl.BlockSpec((None, rows, width), lambda g, tab: (0, tab[T_FWD, g], 0))
    bwd_in = pl.BlockSpec((None, rows, width), lambda g, tab: (1, tab[T_BWD, g], 0))
    state_spec = pl.BlockSpec((1, N_DIR, width), lambda g, tab: (tab[T_SEQ, g], 0, 0))
    grid_spec = pltpu.PrefetchScalarGridSpec(
        num_scalar_prefetch=1,
        grid=(n_steps,),
        in_specs=[fwd_in, fwd_in, bwd_in, bwd_in, state_spec],
        out_specs=[
            pl.BlockSpec((rows, width), _chunk_map(T_FWD)),
            pl.BlockSpec((rows, width), _chunk_map(T_BWD)),
            state_spec,
        ],
        scratch_shapes=[pltpu.VMEM((N_DIR, width), F32)],
    )
    return pl.pallas_call(
        _lru_kernel,
        grid_spec=grid_spec,
        out_shape=[
            jax.ShapeDtypeStruct((t, width), F32),
            jax.ShapeDtypeStruct((t, width), F32),
            jax.ShapeDtypeStruct((n_seq, N_DIR, width), F32),
        ],
        compiler_params=_params(("arbitrary",)),
        name="lru_scan",
    )(tab, a, u, a, u, h0)


def _postmix_kernel(yf_ref, yb_ref, bf_ref, bb_ref, gd_ref, hf_ref, hb_ref, lg_ref, mg_ref,
                    x_ref, mod_ref, lnw_ref, lnb_ref, gup_ref, wa_ref, wb_ref, wo_ref,
                    gpost_ref, gpre_ref, e_ref, et_ref, *rest, d_model, n_experts):
    if n_experts:
        router_ref, x1_ref, h2_ref, gates_ref = rest
    else:
        x1_ref, h2_ref = rest
    y = yf_ref[...] + yb_ref[...]
    inv_d = 1.0 / HEAD_DIM
    mean = _seg_sum(y, e_ref, et_ref) * inv_d
    yc = y - mean
    var = _seg_sum(yc * yc, e_ref, et_ref) * inv_d
    yn = yc * lax.rsqrt(var + LNX_EPS) * lnw_ref[...] + lnb_ref[...]
    gate = _dot(_sigmoid(gd_ref[...]).astype(BF16), gup_ref[...])
    ya = (yn + bf_ref[...] + bb_ref[...]) * gate
    yl = (hf_ref[...] + hb_ref[...]) * lg_ref[...]
    proj_a = _dot(ya.astype(BF16), wa_ref[...])
    proj_b = _dot(yl.astype(BF16), wb_ref[...])
    m = mg_ref[:, :d_model] * proj_a + mg_ref[:, d_model:] * proj_b
    o = _dot(m.astype(BF16), wo_ref[...])
    x1 = x_ref[...] + mod_ref[0, 2:3, :] * _rmsnorm(o, gpost_ref[...])
    x1_ref[...] = x1
    h2 = _rmsnorm(x1, gpre_ref[...]) * (1.0 + mod_ref[0, 4:5, :]) + mod_ref[0, 3:4, :]
    h2_ref[...] = h2.astype(BF16)
    if n_experts:
        lane = lax.broadcasted_iota(jnp.int32, (1, LANES), 1).astype(F32)
        logits = jnp.where(lane < n_experts, _dot3(h2, router_ref[...]), -jnp.inf)
        m1 = jnp.max(logits, axis=-1, keepdims=True)
        i1 = jnp.min(jnp.where(logits == m1, lane, float(LANES)), axis=-1, keepdims=True)
        rest_l = jnp.where(lane == i1, -jnp.inf, logits)
        m2 = jnp.max(rest_l, axis=-1, keepdims=True)
        i2 = jnp.min(jnp.where(rest_l == m2, lane, float(LANES)), axis=-1, keepdims=True)
        e2 = jnp.exp(m2 - m1)
        w1 = 1.0 / (1.0 + e2)
        gates = jnp.where(lane == i1, w1, 0.0) + jnp.where(lane == i2, e2 * w1, 0.0)
        gates_ref[...] = gates.T[:SUBLANES]


def _postmix(yf, yb, bf, bb, pa, hf, hb, pb, pg, x, mod_t, lnw, lnb, gup, wa, wb, wo, gpost,
             gpre, e, et, router, *, tm, gd_block):
    t, d = x.shape
    width = yf.shape[1]
    lru_w = hf.shape[1]
    n_experts = 0 if router is None else router.shape[1]
    row = lambda i: (i, 0)
    const = lambda i: (0, 0)
    vec_d = pl.BlockSpec((1, d), const)
    vec_w = pl.BlockSpec((1, width), const)
    tile_w = pl.BlockSpec((tm, width), row)
    tile_l = pl.BlockSpec((tm, lru_w), row)
    in_specs = [
        tile_w, tile_w, tile_w, tile_w,
        pl.BlockSpec((tm, LANES), lambda i: (i, gd_block)),
        tile_l, tile_l, tile_l,
        pl.BlockSpec((tm, 2 * d), row),
        pl.BlockSpec((tm, d), row),
        pl.BlockSpec((1, N_MOD, d), lambda i: (i, 0, 0)),
        vec_w, vec_w,
        pl.BlockSpec((LANES, width), const),
        pl.BlockSpec((width, d), const),
        pl.BlockSpec((lru_w, d), const),
        pl.BlockSpec((d, d), const),
        vec_d, vec_d,
        pl.BlockSpec((width, LANES), const),
        pl.BlockSpec((LANES, width), const),
    ]
    args = [yf, yb, bf, bb, pa, hf, hb, pb, pg, x, mod_t, lnw, lnb, gup, wa, wb, wo, gpost,
            gpre, e, et]
    out_specs = [pl.BlockSpec((tm, d), row), pl.BlockSpec((tm, d), row)]
    out_shape = [jax.ShapeDtypeStruct((t, d), F32), jax.ShapeDtypeStruct((t, d), BF16)]
    if n_experts:
        assert n_experts <= SUBLANES
        router_pad = jnp.zeros((d, LANES), F32).at[:, :n_experts].set(router)
        in_specs.append(pl.BlockSpec((d, LANES), const))
        args.append(router_pad)
        out_specs.append(pl.BlockSpec((SUBLANES, tm), lambda i: (0, i)))
        out_shape.append(jax.ShapeDtypeStruct((SUBLANES, t), F32))
    return pl.pallas_call(
        functools.partial(_postmix_kernel, d_model=d, n_experts=n_experts),
        grid=(t // tm,),
        in_specs=in_specs,
        out_specs=out_specs,
        out_shape=out_shape,
        compiler_params=_params(("arbitrary",)),
        name="postmix",
    )(*args)


def _ffn_kernel(h_ref, x1_ref, mod_ref, gpost_ref, w1_ref, w3_ref, w2_ref, o_ref, acc_ref):
    f = pl.program_id(1)

    @pl.when(f == 0)
    def _():
        acc_ref[...] = jnp.zeros_like(acc_ref)

    h = h_ref[...]
    hid = _silu(_dot(h, w1_ref[...])) * _dot(h, w3_ref[...])
    acc_ref[...] += _dot(hid.astype(BF16), w2_ref[...])

    @pl.when(f == pl.num_programs(1) - 1)
    def _():
        o_ref[...] = x1_ref[...] + mod_ref[0, 5:6, :] * _rmsnorm(acc_ref[...], gpost_ref[...])


def _ffn(h2, x1, mod_t, gpost, w1, w3, w2, *, tm, tf):
    t, d = x1.shape
    ff = w1.shape[1]
    row = lambda i, f: (i, 0)
    return pl.pallas_call(
        _ffn_kernel,
        grid=(t // tm, ff // tf),
        in_specs=[
            pl.BlockSpec((tm, d), row),
            pl.BlockSpec((tm, d), row),
            pl.BlockSpec((1, N_MOD, d), lambda i, f: (i, 0, 0)),
            pl.BlockSpec((1, d), lambda i, f: (0, 0)),
            pl.BlockSpec((d, tf), lambda i, f: (0, f)),
            pl.BlockSpec((d, tf), lambda i, f: (0, f)),
            pl.BlockSpec((tf, d), lambda i, f: (f, 0)),
        ],
        out_specs=pl.BlockSpec((tm, d), row),
        out_shape=jax.ShapeDtypeStruct((t, d), F32),
        scratch_shapes=[pltpu.VMEM((tm, d), F32)],
        compiler_params=_params(("arbitrary", "arbitrary")),
        name="ffn_dense",
    )(h2, x1, mod_t, gpost, w1, w3, w2)


def _moe_kernel(gt_ref, before_ref, h_ref, x1_ref, mod_ref, gpost_ref, w1_ref, w3_ref, w2_ref,
                o_ref, pos_scr, x_scr, y_scr):
    e = pl.program_id(1)
    f = pl.program_id(2)
    last_f = pl.num_programs(2) - 1

    @pl.when((e == 0) & (f == 0))
    def _():
        chosen_all = jnp.where(gt_ref[...] > 0.0, 1.0, 0.0).astype(BF16)
        pos_scr[...] = _dot(chosen_all, before_ref[...])
        o_ref[...] = jnp.zeros_like(o_ref)

    is_e = lax.broadcasted_iota(jnp.int32, (SUBLANES, 1), 0) == e
    gate_row = jnp.sum(jnp.where(is_e, gt_ref[...], 0.0), axis=0, keepdims=True)
    pos_row = jnp.sum(jnp.where(is_e, pos_scr[...], 0.0), axis=0, keepdims=True)
    chosen = gate_row > 0.0
    slot_row = jnp.where(chosen, pos_row, -1.0)
    count = jnp.sum(jnp.where(chosen, 1.0, 0.0)).astype(jnp.int32)
    n_sub = (count + (MOE_SUB - 1)) // MOE_SUB

    def body(j, carry):
        r0 = pl.multiple_of(j * MOE_SUB, MOE_SUB)
        slots = (r0 + lax.broadcasted_iota(jnp.int32, (MOE_SUB, 1), 0)).astype(F32)
        onehot = slot_row == slots
        p = jnp.where(onehot, 1.0, 0.0).astype(BF16)

        @pl.when(f == 0)
        def _():
            x_scr[pl.ds(r0, MOE_SUB), :] = _dot(p, h_ref[...]).astype(BF16)

        x = x_scr[pl.ds(r0, MOE_SUB), :]
        hid = _silu(_dot(x, w1_ref[...])) * _dot(x, w3_ref[...])
        g_col = jnp.sum(jnp.where(onehot, gate_row, 0.0), axis=1, keepdims=True)
        part = _dot((hid * g_col).astype(BF16), w2_ref[...])

        @pl.when(f == 0)
        def _():
            y_scr[pl.ds(r0, MOE_SUB), :] = part

        @pl.when(f > 0)
        def _():
            y_scr[pl.ds(r0, MOE_SUB), :] += part

        @pl.when(f == last_f)
        def _():
            hi, lo = _split2(y_scr[pl.ds(r0, MOE_SUB), :])
            o_ref[...] += _dot_tn(jnp.concatenate([p, p], axis=0),
                                  jnp.concatenate([hi, lo], axis=0))
        return carry

    lax.fori_loop(0, n_sub, body, 0)

    @pl.when((e == pl.num_programs(1) - 1) & (f == last_f))
    def _():
        o_ref[...] = x1_ref[...] + mod_ref[0, 5:6, :] * _rmsnorm(o_ref[...], gpost_ref[...])


def _moe(gates_t, h2, x1, mod_t, gpost, w1, w3, w2, *, tb, tf):
    t, d = x1.shape
    n_e, _, ff = w1.shape
    before = jnp.asarray(np.triu(np.ones((tb, tb), np.float32), 1), BF16)
    row = lambda b, e, f: (b, 0)
    return pl.pallas_call(
        _moe_kernel,
        grid=(t // tb, n_e, ff // tf),
        in_specs=[
            pl.BlockSpec((SUBLANES, tb), lambda b, e, f: (0, b)),
            pl.BlockSpec((tb, tb), lambda b, e, f: (0, 0)),
            pl.BlockSpec((tb, d), row),
            pl.BlockSpec((tb, d), row),
            pl.BlockSpec((1, N_MOD, d), lambda b, e, f: (b, 0, 0)),
            pl.BlockSpec((1, d), lambda b, e, f: (0, 0)),
            pl.BlockSpec((None, d, tf), lambda b, e, f: (e, 0, f)),
            pl.BlockSpec((None, d, tf), lambda b, e, f: (e, 0, f)),
            pl.BlockSpec((None, tf, d), lambda b, e, f: (e, f, 0)),
        ],
        out_specs=pl.BlockSpec((tb, d), row),
        out_shape=jax.ShapeDtypeStruct((t, d), F32),
        scratch_shapes=[pltpu.VMEM((SUBLANES, tb), F32), pltpu.VMEM((tb, d), BF16),
                        pltpu.VMEM((tb, d), F32)],
        compiler_params=_params(("arbitrary", "arbitrary", "arbitrary")),
        name="ffn_moe",
    )(gates_t, before, h2, x1, mod_t, gpost, w1, w3, w2)


def _head_indicator(width):
    e = np.zeros((width, LANES), np.float32)
    e[np.arange(width), np.arange(width) // HEAD_DIM] = 1.0
    return jnp.asarray(e, BF16), jnp.asarray(e.T.copy(), BF16)


def _tile_mod(mod_l, tile_rows, tm, d):
    idx = np.asarray(tile_rows[::tm], np.int32)
    return jnp.take(mod_l, idx, axis=0).reshape(len(idx), N_MOD, d)


def _pick_tile(t, pref):
    while t % pref:
        pref //= 2
    return pref


def kernel(x_prompt, x_sample, c, c_ctx, state_rwkv, state_lru, mod_w, mod_b, norm_pre_mix, norm_post_mix, norm_pre_ffn, norm_post_ffn, w_in, b_merge, rwkv_mu, rwkv_w0, rwkv_w_up, rwkv_a0, rwkv_a_up, rwkv_k_k, rwkv_k_a, rwkv_r_k, rwkv_g_up, rwkv_lnx_w, rwkv_lnx_b, lru_conv_w, lru_conv_b, lru_w_rg, lru_b_rg, lru_w_ig, lru_b_ig, lru_lam, w_proj_a, w_proj_b, w_out, ffn_w1, ffn_w3, ffn_w2, moe_router, moe_w1, moe_w3, moe_w2):
    b_p, seq, d = x_prompt.shape
    b_s, dec_seq, _ = x_sample.shape
    depth = w_in.shape[0]
    n_heads, head_dim = rwkv_r_k.shape[1], rwkv_r_k.shape[2]
    width = n_heads * head_dim
    n_shift = rwkv_mu.shape[2]
    lru_w = lru_lam.shape[2]
    blk = lru_w_rg.shape[3]
    gate_lora = rwkv_g_up.shape[1]
    lora = rwkv_w_up.shape[2]
    na = n_shift + gate_lora
    nb = 2 * lru_w
    assert head_dim == HEAD_DIM and blk == LANES and rwkv_mu.shape[1] == N_DIR
    assert n_shift == 3 * width + 2 * lora and 2 * lora == LANES and gate_lora == LANES
    assert dec_seq % GRID_W == 0 and lru_conv_w.shape[1] == 4 and width % GROUP == 0

    t_p, t_s = b_p * seq, b_s * dec_seq
    x = jnp.concatenate([x_prompt.reshape(t_p, d), x_sample.reshape(t_s, d)], axis=0)

    n_cond = -(-(1 + b_s) // 16) * 16
    cond = jnp.zeros((n_cond, d), F32).at[0].set(c_ctx).at[1:1 + b_s].set(c)
    mod = _modulation(cond, mod_w, mod_b)
    tile_rows = np.concatenate([np.zeros(t_p, np.int32),
                                1 + np.arange(t_s, dtype=np.int32) // dec_seq])

    seq_lens, row_lens = [seq] * b_p + [dec_seq] * b_s, [seq] * b_p + [GRID_W] * b_s
    tab = _scan_table(seq_lens, row_lens, CHUNK)
    lru_rows = _pick_tile(np.gcd(seq, dec_seq), 256)
    lru_tab = _scan_table(seq_lens, row_lens, lru_rows)
    e_ind, et_ind = _head_indicator(width)

    tm = _pick_tile(np.gcd(seq, dec_seq), 256)
    tap_masks = _conv_tap_masks(tm, [seq, GRID_W])
    tm_ffn = _pick_tile(np.gcd(t_p, dec_seq), 512)
    tb_moe = _pick_tile(np.gcd(t_p, dec_seq), 1024)

    rwkv_fin, lru_fin = [], []
    for l in range(depth):
        mod_t = _tile_mod(mod[l], tile_rows, tm, d)
        wg = jnp.concatenate([lru_w_rg[l], lru_w_ig[l]], axis=-1).astype(BF16)
        pa, lgate, pg, lru_a, lru_u = _inproj(
            x, mod_t, norm_pre_mix[l][None], b_merge[l][None], w_in[l].astype(BF16),
            tap_masks, t_p // tm, lru_conv_w[l], lru_conv_b[l][None], wg, lru_b_rg[l][:, None],
            lru_b_ig[l][:, None], lru_lam[l][:, None], na, nb, tm)

        s0 = jnp.concatenate(
            [jnp.zeros((b_p, N_DIR, width, HEAD_DIM), F32),
             state_rwkv[:, l].astype(F32).reshape(b_s, N_DIR, width, HEAD_DIM)], axis=0)
        zpad = jnp.zeros((N_DIR, lora, width), F32)
        wup = jnp.concatenate([rwkv_w_up[l], zpad], axis=1).astype(BF16)
        aup = jnp.concatenate([zpad, rwkv_a_up[l]], axis=1).astype(BF16)
        yf, bonf, yb, bonb, sfin = _rwkv_scan(
            tab, pa, s0, rwkv_mu[l][:, None], rwkv_w0[l][:, None], wup, rwkv_a0[l][:, None], aup,
            rwkv_k_k[l][None], rwkv_k_a[l][None], rwkv_r_k[l].reshape(1, width), e_ind, et_ind,
            width=width, n_shift=n_shift)
        rwkv_fin.append(sfin[:b_p].reshape(b_p, N_DIR, n_heads, HEAD_DIM, HEAD_DIM))

        h0 = jnp.concatenate([jnp.zeros((b_p, N_DIR, lru_w), F32),
                              state_lru[:, l].astype(F32)], axis=0)
        hf, hb, hfin = _lru_scan(lru_tab, lru_a, lru_u, h0, lru_rows)
        lru_fin.append(hfin[:b_p])

        is_moe = l % 2 == 1
        i = l // 2
        post = _postmix(
            yf, yb, bonf, bonb, pa, hf, hb, lgate, pg, x, mod_t,
            rwkv_lnx_w[l][None], rwkv_lnx_b[l][None], rwkv_g_up[l].astype(BF16),
            w_proj_a[l].astype(BF16), w_proj_b[l].astype(BF16), w_out[l].astype(BF16),
            norm_post_mix[l][None], norm_pre_ffn[l][None], e_ind, et_ind,
            moe_router[i] if is_moe else None, tm=tm, gd_block=n_shift // LANES)
        if is_moe:
            x1, h2, gates_t = post
            ff = moe_w1.shape[3]
            x = _moe(gates_t, h2, x1, _tile_mod(mod[l], tile_rows, tb_moe, d),
                     norm_post_ffn[l][None], moe_w1[i].astype(BF16), moe_w3[i].astype(BF16),
                     moe_w2[i].astype(BF16), tb=tb_moe,
                     tf=ff // 2 if ff % (2 * LANES) == 0 else ff)
        else:
            x1, h2 = post
            ff = ffn_w1.shape[2]
            x = _ffn(h2, x1, _tile_mod(mod[l], tile_rows, tm_ffn, d), norm_post_ffn[l][None],
                     ffn_w1[i].astype(BF16), ffn_w3[i].astype(BF16), ffn_w2[i].astype(BF16),
                     tm=tm_ffn, tf=_pick_tile(ff, 1024))

    y_prompt = x[:t_p].reshape(b_p, seq, d).astype(x_prompt.dtype)
    y_sample = x[t_p:].reshape(b_s, dec_seq, d).astype(x_sample.dtype)
    new_state_rwkv = jnp.stack(rwkv_fin, axis=1).astype(x_prompt.dtype)
    new_state_lru = jnp.stack(lru_fin, axis=1).astype(x_prompt.dtype)
    return (y_prompt, y_sample, new_state_rwkv, new_state_lru)
```
